```python
import math, functools
import jax, jax.numpy as jnp
from jax import lax
import numpy as np

D_MODEL = 4096
BATCH = 4
SEQ = 2048
DEPTH = 2
DEC_BATCH = 128
DEC_SEQ = 8
PAST_LEN = 16384
PAGE_SIZE = 128

MLA_HEADS = 16
MLA_Q_RANK = 1024
MLA_KV_RANK = 256
MLA_NOPE = 128
MLA_ROPE = 64
MLA_V = 128
MLA_ROW = MLA_KV_RANK + MLA_ROPE
ROPE_THETA = 10000.0
DIFF_HEADS = 16
DIFF_KV_HEADS = 1
DIFF_DH = 64
DIFF_DV = 2 * DIFF_DH
DIFF_LAMBDA_INIT = 0.8 - 0.6 * math.exp(-0.3 * 0)
MOBA_HEADS = 32
MOBA_KV_HEADS = 2
MOBA_DH = 128
MOBA_BLOCK = 256
MOBA_TOPK = 3
D_FF = 4 * D_MODEL
AB_IN = MLA_Q_RANK + MLA_ROW + DIFF_HEADS * 2 * DIFF_DH + DIFF_KV_HEADS * (2 * DIFF_DH + DIFF_DV)
AB_OUT = MLA_HEADS * MLA_V + DIFF_HEADS * DIFF_DV
C_IN = (MOBA_HEADS + 2 * MOBA_KV_HEADS) * MOBA_DH
C_OUT = MOBA_HEADS * MOBA_DH
Q_BLOCK = 128
MOBA_Q_CHUNK = 8
EPS = 1e-6
F32 = jnp.float32

kernel_name = 'hybrid_mla_diff_moba_adaln_decode_step'


def _rmsnorm(x, g):
    xf = x.astype(F32)
    y = xf * lax.rsqrt(jnp.mean(xf * xf, axis=-1, keepdims=True) + EPS)
    return (y * g.astype(F32)).astype(x.dtype)


def _alibi_slopes(n):
    return jnp.asarray(2.0 ** (-8.0 * np.arange(1, n + 1) / n), dtype=F32)


def _rope(x, pos):
    half = x.shape[-1] // 2
    inv = ROPE_THETA ** (-jnp.arange(half, dtype=F32) / half)
    ang = pos.astype(F32)[..., None] * inv
    cos, sin = jnp.cos(ang).astype(x.dtype), jnp.sin(ang).astype(x.dtype)
    x1, x2 = x[..., :half], x[..., half:]
    return jnp.concatenate([x1 * cos - x2 * sin, x2 * cos + x1 * sin], axis=-1)


def _ada(c, w, b):
    m = jax.nn.silu(c) @ w + b
    return [a[:, None, :] for a in jnp.split(m, 6, axis=-1)]


def _pre(x, g, shift, scale):
    return _rmsnorm(x, g) * (1.0 + scale) + shift


def _post(x, y, g, gate):
    return x + gate * _rmsnorm(y, g)


def _ffn(h, w_up, w_down):
    return jnp.square(jax.nn.relu(h @ w_up)) @ w_down


def _attn_probs(q, k, qpos, kpos, scale, slopes):
    tq, h, d = q.shape
    s_len, g, _ = k.shape
    s = jnp.einsum('tgrd,sgd->tgrs', q.reshape(tq, g, h // g, d), k,
                   preferred_element_type=F32).reshape(tq, h, s_len) * scale
    dist = (qpos[:, None] - kpos[None, :]).astype(F32)
    if slopes is not None:
        s = s - slopes[None, :, None] * dist[:, None, :]
    s = jnp.where(dist[:, None, :] >= 0, s, -jnp.inf)
    return jax.nn.softmax(s, axis=-1)


def _mla_core(q, qpos, rows, kpos):
    p = _attn_probs(q, rows[:, None, :], qpos, kpos, (MLA_NOPE + MLA_ROPE) ** -0.5, None)
    return jnp.einsum('ths,sc->thc', p.astype(rows.dtype), rows[:, :MLA_KV_RANK])


def _diff_core(q, qpos, k, v, kpos, lam):
    slopes = _alibi_slopes(DIFF_HEADS)
    scale = DIFF_DH ** -0.5
    p1 = _attn_probs(q[..., :DIFF_DH], k[..., :DIFF_DH], qpos, kpos, scale, slopes)
    p2 = _attn_probs(q[..., DIFF_DH:], k[..., DIFF_DH:], qpos, kpos, scale, slopes)
    a = p1 - lam * p2
    tq, h, s_len = a.shape
    g = v.shape[1]
    o = jnp.einsum('tgrs,sgv->tgrv', a.reshape(tq, g, h // g, s_len).astype(v.dtype), v)
    return o.reshape(tq, h, DIFF_DV)


def _ab_attend(qs, qpos, kvs, kpos, lam):
    q_mla, q_diff = qs
    rows, k_diff, v_diff = kvs
    return (_mla_core(q_mla, qpos, rows, kpos), _diff_core(q_diff, qpos, k_diff, v_diff, kpos, lam))


def _prompt_sweep(fn, qs, kvs):
    b, t = qs[0].shape[:2]
    nb = t // Q_BLOCK
    kpos = jnp.arange(t, dtype=jnp.int32)
    qpos_b = kpos.reshape(nb, Q_BLOCK)
    qs_b = tuple(jnp.swapaxes(a.reshape(b, nb, Q_BLOCK, *a.shape[2:]), 0, 1) for a in qs)
    per_seq = jax.vmap(fn, in_axes=(0, None, 0, None))

    def body(blk):
        q_blk, qp = blk
        return per_seq(q_blk, qp, kvs, kpos)

    out = lax.map(body, (qs_b, qpos_b))
    return jax.tree_util.tree_map(lambda o: jnp.swapaxes(o, 0, 1).reshape(b, t, *o.shape[3:]), out)


def _sample_sweep(fn, qs, new_kvs, pools, page_table):
    t = qs[0].shape[1]
    qpos = PAST_LEN + jnp.arange(t, dtype=jnp.int32)
    kpos = jnp.arange(PAST_LEN + t, dtype=jnp.int32)

    def body(args):
        q, new, pages = args
        kv = tuple(jnp.concatenate([pool[pages].reshape(PAST_LEN, *pool.shape[2:]), n], axis=0)
                   for pool, n in zip(pools, new))
        return fn(q, qpos, kv, kpos)

    return lax.map(body, (qs, new_kvs, page_table))


def _ab_project(h, pos, w_in_ab, g_q_a, w_q_b, g_kv_a, w_uk):
    b, t, _ = h.shape
    z = h @ w_in_ab
    o1 = MLA_Q_RANK
    o2 = o1 + MLA_ROW
    o3 = o2 + DIFF_HEADS * 2 * DIFF_DH
    o4 = o3 + DIFF_KV_HEADS * 2 * DIFF_DH
    q_a, kv_a, dq, dk, dv = z[..., :o1], z[..., o1:o2], z[..., o2:o3], z[..., o3:o4], z[..., o4:]
    q = (_rmsnorm(q_a, g_q_a) @ w_q_b).reshape(b, t, MLA_HEADS, MLA_NOPE + MLA_ROPE)
    q_lat = jnp.einsum('bthn,chn->bthc', q[..., :MLA_NOPE], w_uk)
    q_pe = _rope(q[..., MLA_NOPE:], pos[:, None])
    q_mla = jnp.concatenate([q_lat, q_pe], axis=-1)
    rows = jnp.concatenate([_rmsnorm(kv_a[..., :MLA_KV_RANK], g_kv_a),
                            _rope(kv_a[..., MLA_KV_RANK:], pos)], axis=-1)
    q_diff = dq.reshape(b, t, DIFF_HEADS, 2 * DIFF_DH)
    k_diff = dk.reshape(b, t, DIFF_KV_HEADS, 2 * DIFF_DH)
    v_diff = dv.reshape(b, t, DIFF_KV_HEADS, DIFF_DV)
    return (q_mla, q_diff), (rows, k_diff, v_diff)


def _ab_output(o_lat, o_diff, w_uv, g_diff_sub, w_out_ab):
    b, t = o_lat.shape[:2]
    o_mla = jnp.einsum('bthc,chv->bthv', o_lat, w_uv).reshape(b, t, MLA_HEADS * MLA_V)
    o_d = (_rmsnorm(o_diff, g_diff_sub) * (1.0 - DIFF_LAMBDA_INIT)).reshape(b, t, DIFF_HEADS * DIFF_DV)
    return jnp.concatenate([o_mla, o_d], axis=-1) @ w_out_ab


def _c_project(h, w_in_c):
    b, t, _ = h.shape
    z = h @ w_in_c
    nq = MOBA_HEADS * MOBA_DH
    nk = MOBA_KV_HEADS * MOBA_DH
    return (z[..., :nq].reshape(b, t, MOBA_HEADS, MOBA_DH),
            z[..., nq:nq + nk].reshape(b, t, MOBA_KV_HEADS, MOBA_DH),
            z[..., nq + nk:].reshape(b, t, MOBA_KV_HEADS, MOBA_DH))


def _moba_core(q, qpos, sel_k, sel_v, sel_idx, sel_valid, own_k, own_v, own_pos):
    tq, h, d = q.shape
    ns = sel_k.shape[2]
    g = own_k.shape[1]
    slopes = _alibi_slopes(h)
    scale = d ** -0.5
    sel_pos = sel_idx[..., None] * MOBA_BLOCK + jnp.arange(MOBA_BLOCK, dtype=jnp.int32)
    d_sel = (qpos[:, None, None, None] - sel_pos).astype(F32)
    s_sel = (jnp.einsum('thd,thnsd->thns', q, sel_k, preferred_element_type=F32) * scale
             - slopes[None, :, None, None] * d_sel)
    s_sel = jnp.where(sel_valid[..., None], s_sel, -jnp.inf).reshape(tq, h, ns * MOBA_BLOCK)
    d_own = (qpos[:, None] - own_pos[None, :]).astype(F32)
    s_own = (jnp.einsum('tgrd,lgd->tgrl', q.reshape(tq, g, h // g, d), own_k,
                        preferred_element_type=F32).reshape(tq, h, own_k.shape[0]) * scale
             - slopes[None, :, None] * d_own[:, None, :])
    s_own = jnp.where(d_own[:, None, :] >= 0, s_own, -jnp.inf)
    p = jax.nn.softmax(jnp.concatenate([s_sel, s_own], axis=-1), axis=-1)
    p_sel = p[..., :ns * MOBA_BLOCK].reshape(tq, h, ns, MOBA_BLOCK).astype(sel_v.dtype)
    p_own = p[..., ns * MOBA_BLOCK:].reshape(tq, g, h // g, own_k.shape[0]).astype(own_v.dtype)
    o_sel = jnp.einsum('thns,thnsd->thd', p_sel, sel_v)
    o_own = jnp.einsum('tgrl,lgd->tgrd', p_own, own_v).reshape(tq, h, d)
    return o_sel + o_own


def _moba_prompt(q, k, v):
    b, t, h, d = q.shape
    g = k.shape[2]
    r = h // g
    nblk = -(-t // MOBA_BLOCK)
    pad = nblk * MOBA_BLOCK - t
    kb = jnp.pad(k, ((0, 0), (0, pad), (0, 0), (0, 0))).reshape(b, nblk, MOBA_BLOCK, g, d)
    vb = jnp.pad(v, ((0, 0), (0, pad), (0, 0), (0, 0))).reshape(b, nblk, MOBA_BLOCK, g, d)
    qblk = jnp.arange(t, dtype=jnp.int32) // MOBA_BLOCK
    n_sel = min(MOBA_TOPK, nblk - 1)
    if n_sel > 0:
        kmean = jnp.mean(kb.astype(F32), axis=2)
        gate = jnp.einsum('btgrd,bjgd->btgrj', q.reshape(b, t, g, r, d).astype(F32), kmean).reshape(b, t, h, nblk)
        gate = jnp.where(jnp.arange(nblk)[None, None, None, :] < qblk[None, :, None, None], gate, -jnp.inf)
        _, idx = lax.top_k(gate, n_sel)
    else:
        idx = jnp.zeros((b, t, h, 0), jnp.int32)
    idx = idx.astype(jnp.int32)
    valid = idx < qblk[None, :, None, None]
    nc = t // MOBA_Q_CHUNK
    chunk = lambda a: jnp.swapaxes(a.reshape(b, nc, MOBA_Q_CHUNK, *a.shape[2:]), 0, 1)
    bi = jnp.arange(b)[:, None, None, None]
    gi = (jnp.arange(h) // r)[None, None, :, None]
    core = jax.vmap(_moba_core, in_axes=(0, None, 0, 0, 0, 0, 0, 0, None))

    def body(args):
        qc, ic, vc, ci = args
        qpos = ci * MOBA_Q_CHUNK + jnp.arange(MOBA_Q_CHUNK, dtype=jnp.int32)
        ob = (ci * MOBA_Q_CHUNK) // MOBA_BLOCK
        sel_k = kb[bi, ic, :, gi, :]
        sel_v = vb[bi, ic, :, gi, :]
        own_k = lax.dynamic_index_in_dim(kb, ob, axis=1, keepdims=False)
        own_v = lax.dynamic_index_in_dim(vb, ob, axis=1, keepdims=False)
        own_pos = ob * MOBA_BLOCK + jnp.arange(MOBA_BLOCK, dtype=jnp.int32)
        return core(qc, qpos, sel_k, sel_v, ic, vc, own_k, own_v, own_pos)

    out = lax.map(body, (chunk(q), chunk(idx), chunk(valid), jnp.arange(nc, dtype=jnp.int32)))
    return jnp.swapaxes(out, 0, 1).reshape(b, t, h, d)


def _moba_sample(q, k_new, v_new, page_table, pool_k, pool_v):
    db, t, h, d = q.shape
    g = k_new.shape[2]
    r = h // g
    ppb = MOBA_BLOCK // PAGE_SIZE
    nfull = PAST_LEN // MOBA_BLOCK
    n_sel = min(MOBA_TOPK, nfull)
    n_own_pages = (PAST_LEN - nfull * MOBA_BLOCK) // PAGE_SIZE
    qpos = PAST_LEN + jnp.arange(t, dtype=jnp.int32)
    own_pos = nfull * MOBA_BLOCK + jnp.arange(n_own_pages * PAGE_SIZE + t, dtype=jnp.int32)
    gi = (jnp.arange(h) // r)[None, :, None, None]

    def body(args):
        qs, kn, vn, pages = args
        if n_sel > 0:
            k_full = pool_k[pages[:nfull * ppb]].reshape(nfull, MOBA_BLOCK, g, d)
            kmean = jnp.mean(k_full.astype(F32), axis=1)
            gate = jnp.einsum('tgrd,jgd->tgrj', qs.reshape(t, g, r, d).astype(F32), kmean).reshape(t, h, nfull)
            _, idx = lax.top_k(gate, n_sel)
        else:
            idx = jnp.zeros((t, h, 0), jnp.int32)
        idx = idx.astype(jnp.int32)
        phys = pages[idx[..., None] * ppb + jnp.arange(ppb, dtype=jnp.int32)]
        sel_k = pool_k[phys, :, gi, :].reshape(t, h, n_sel, MOBA_BLOCK, d)
        sel_v = pool_v[phys, :, gi, :].reshape(t, h, n_sel, MOBA_BLOCK, d)
        own_pages = pages[nfull * ppb:]
        own_k = jnp.concatenate([pool_k[own_pages].reshape(n_own_pages * PAGE_SIZE, g, d), kn], axis=0)
        own_v = jnp.concatenate([pool_v[own_pages].reshape(n_own_pages * PAGE_SIZE, g, d), vn], axis=0)
        valid = jnp.ones(idx.shape, dtype=bool)
        return _moba_core(qs, qpos, sel_k, sel_v, idx, valid, own_k, own_v, own_pos)

    return lax.map(body, (q, k_new, v_new, page_table))


def setup_inputs(seed: int = 0) -> dict:
    key = jax.random.key(seed)
    ki = iter(list(jax.random.split(key, 40)))

    def nrm(shape, scale=1.0):
        return jax.random.normal(next(ki), shape, F32) * scale

    def gain(shape):
        return 1.0 + nrm(shape, 0.02)

    n_pages = PAST_LEN // PAGE_SIZE
    n_pool = (DEC_BATCH * n_pages * 5) // 4
    page_table = jax.random.permutation(next(ki), n_pool)[:DEC_BATCH * n_pages]
    page_table = page_table.reshape(DEC_BATCH, n_pages).astype(jnp.int32)
    return {
        'x_prompt': nrm((BATCH, SEQ, D_MODEL)),
        'x_sample': nrm((DEC_BATCH, DEC_SEQ, D_MODEL)),
        'cache_mla': nrm((n_pool, PAGE_SIZE, MLA_ROW)),
        'cache_diff_k': nrm((n_pool, PAGE_SIZE, DIFF_KV_HEADS, 2 * DIFF_DH)),
        'cache_diff_v': nrm((n_pool, PAGE_SIZE, DIFF_KV_HEADS, DIFF_DV)),
        'cache_moba_k': nrm((n_pool, PAGE_SIZE, MOBA_KV_HEADS, MOBA_DH)),
        'cache_moba_v': nrm((n_pool, PAGE_SIZE, MOBA_KV_HEADS, MOBA_DH)),
        'page_table': page_table,
        'c_prompt': nrm((BATCH, D_MODEL)),
        'c_sample': nrm((DEC_BATCH, D_MODEL)),
        'w_ada': nrm((DEPTH, D_MODEL, 6 * D_MODEL), D_MODEL ** -0.5),
        'b_ada': nrm((DEPTH, 6 * D_MODEL), 0.02),
        'g_pre_mix': gain((DEPTH, D_MODEL)),
        'g_post_mix': gain((DEPTH, D_MODEL)),
        'g_pre_ffn': gain((DEPTH, D_MODEL)),
        'g_post_ffn': gain((DEPTH, D_MODEL)),
        'w_in_ab': nrm((D_MODEL, AB_IN), D_MODEL ** -0.5),
        'g_q_a': gain((MLA_Q_RANK,)),
        'w_q_b': nrm((MLA_Q_RANK, MLA_HEADS * (MLA_NOPE + MLA_ROPE)), MLA_Q_RANK ** -0.5),
        'g_kv_a': gain((MLA_KV_RANK,)),
        'w_uk': nrm((MLA_KV_RANK, MLA_HEADS, MLA_NOPE), MLA_KV_RANK ** -0.5),
        'w_uv': nrm((MLA_KV_RANK, MLA_HEADS, MLA_V), MLA_KV_RANK ** -0.5),
        'lambda_q1': nrm((DIFF_DH,), 0.1),
        'lambda_k1': nrm((DIFF_DH,), 0.1),
        'lambda_q2': nrm((DIFF_DH,), 0.1),
        'lambda_k2': nrm((DIFF_DH,), 0.1),
        'g_diff_sub': gain((DIFF_DV,)),
        'w_out_ab': nrm((AB_OUT, D_MODEL), AB_OUT ** -0.5),
        'w_in_c': nrm((D_MODEL, C_IN), D_MODEL ** -0.5),
        'w_out_c': nrm((C_OUT, D_MODEL), C_OUT ** -0.5),
        'w_ff_up': nrm((DEPTH, D_MODEL, D_FF), D_MODEL ** -0.5),
        'w_ff_down': nrm((DEPTH, D_FF, D_MODEL), D_FF ** -0.5),
    }


def reference(x_prompt, x_sample, cache_mla, cache_diff_k, cache_diff_v, cache_moba_k, cache_moba_v,
              page_table, c_prompt, c_sample, w_ada, b_ada, g_pre_mix, g_post_mix, g_pre_ffn, g_post_ffn,
              w_in_ab, g_q_a, w_q_b, g_kv_a, w_uk, w_uv, lambda_q1, lambda_k1, lambda_q2, lambda_k2,
              g_diff_sub, w_out_ab, w_in_c, w_out_c, w_ff_up, w_ff_down):
    pos_p = jnp.arange(x_prompt.shape[1], dtype=jnp.int32)
    pos_s = PAST_LEN + jnp.arange(x_sample.shape[1], dtype=jnp.int32)
    lam = (jnp.exp(jnp.sum(lambda_q1.astype(F32) * lambda_k1.astype(F32)))
           - jnp.exp(jnp.sum(lambda_q2.astype(F32) * lambda_k2.astype(F32))) + DIFF_LAMBDA_INIT)
    attend_ab = functools.partial(_ab_attend, lam=lam)
    xp, xs = x_prompt, x_sample
    for l in range(DEPTH):
        sh_p, sc_p, ga_p, sh2_p, sc2_p, ga2_p = _ada(c_prompt, w_ada[l], b_ada[l])
        sh_s, sc_s, ga_s, sh2_s, sc2_s, ga2_s = _ada(c_sample, w_ada[l], b_ada[l])
        hp = _pre(xp, g_pre_mix[l], sh_p, sc_p)
        hs = _pre(xs, g_pre_mix[l], sh_s, sc_s)
        if l % 2 == 0:
            qs_p, kv_p = _ab_project(hp, pos_p, w_in_ab, g_q_a, w_q_b, g_kv_a, w_uk)
            qs_s, kv_s = _ab_project(hs, pos_s, w_in_ab, g_q_a, w_q_b, g_kv_a, w_uk)
            ol_p, od_p = _prompt_sweep(attend_ab, qs_p, kv_p)
            ol_s, od_s = _sample_sweep(attend_ab, qs_s, kv_s, (cache_mla, cache_diff_k, cache_diff_v), page_table)
            mix_p = _ab_output(ol_p, od_p, w_uv, g_diff_sub, w_out_ab)
            mix_s = _ab_output(ol_s, od_s, w_uv, g_diff_sub, w_out_ab)
            mla_p, dk_p, dv_p = kv_p
            mla_s, dk_s, dv_s = kv_s
        else:
            q_p, mk_p, mv_p = _c_project(hp, w_in_c)
            q_s, mk_s, mv_s = _c_project(hs, w_in_c)
            bp, tp = hp.shape[:2]
            bs, ts = hs.shape[:2]
            mix_p = _moba_prompt(q_p, mk_p, mv_p).reshape(bp, tp, C_OUT) @ w_out_c
            mix_s = _moba_sample(q_s, mk_s, mv_s, page_table, cache_moba_k, cache_moba_v).reshape(bs, ts, C_OUT) @ w_out_c
        xp = _post(xp, mix_p, g_post_mix[l], ga_p)
        xs = _post(xs, mix_s, g_post_mix[l], ga_s)
        xp = _post(xp, _ffn(_pre(xp, g_pre_ffn[l], sh2_p, sc2_p), w_ff_up[l], w_ff_down[l]), g_post_ffn[l], ga2_p)
        xs = _post(xs, _ffn(_pre(xs, g_pre_ffn[l], sh2_s, sc2_s), w_ff_up[l], w_ff_down[l]), g_post_ffn[l], ga2_s)
    return (xp, xs, mla_p, dk_p, dv_p, mk_p, mv_p, mla_s, dk_s, dv_s, mk_s, mv_s)
```

```python
import functools
import math

import numpy as np
import jax
import jax.numpy as jnp
from jax import lax
from jax.experimental import pallas as pl
from jax.experimental.pallas import tpu as pltpu

F32 = jnp.float32
BF16 = jnp.bfloat16
EPS = 1e-6
ROPE_THETA = 10000.0
MOBA_BLOCK = 256
MOBA_TOPK = 3
DIFF_LAMBDA_INIT = 0.8 - 0.6 * math.exp(-0.3 * 0)
NEG_INF = float("-inf")
LANES = 128
V7X_VMEM_BYTES = 64 * 1024 * 1024
VMEM_LIMIT = V7X_VMEM_BYTES - 8 * 1024 * 1024
NEW_KEY_PAD = 128


def _cp(*sem):
    return pltpu.CompilerParams(dimension_semantics=sem, vmem_limit_bytes=VMEM_LIMIT)


def _pick(n, pref):
    if n <= pref:
        return n
    d = pref - pref % LANES
    while d >= LANES:
        if n % d == 0:
            return d
        d -= LANES
    return n


def _dot(a, b):
    return jnp.dot(a, b, preferred_element_type=F32)


def _dot_nt(a, b):
    return lax.dot_general(a, b, (((1,), (1,)), ((), ())), preferred_element_type=F32)


def _dot3_nt(a, b):
    ah = a.astype(BF16)
    al = (a - ah.astype(F32)).astype(BF16)
    bh = b.astype(BF16)
    bl = (b - bh.astype(F32)).astype(BF16)
    return _dot_nt(ah, bh) + (_dot_nt(ah, bl) + _dot_nt(al, bh))


def _rms(x):
    return x * lax.rsqrt(jnp.mean(x * x, axis=-1, keepdims=True) + EPS)


def _ada_kernel(c_ref, w_ref, b_ref, o_ref):
    c = c_ref[...]
    a = (c * (1.0 / (1.0 + jnp.exp(-c)))).astype(BF16)
    o_ref[0] = _dot(a, w_ref[0].astype(BF16)) + b_ref[0]


def _ada(c_pad, w_ada, b_ada):
    nl, d, n = w_ada.shape
    bc = c_pad.shape[0]
    bn = _pick(n, 512)
    return pl.pallas_call(
        _ada_kernel,
        out_shape=jax.ShapeDtypeStruct((nl, bc, n), F32),
        grid=(nl, n // bn),
        in_specs=[pl.BlockSpec((bc, d), lambda l, j: (0, 0)),
                  pl.BlockSpec((1, d, bn), lambda l, j: (l, 0, j)),
                  pl.BlockSpec((1, 1, bn), lambda l, j: (l, 0, j))],
        out_specs=pl.BlockSpec((1, bc, bn), lambda l, j: (l, 0, j)),
        compiler_params=_cp("arbitrary", "arbitrary"),
        name="ada",
    )(c_pad, w_ada, b_ada.reshape(nl, 1, n))


def _postpre_kernel(*refs, nblk_p, has_y, has_h, gate_idx, pre_idx):
    it = iter(refs)
    x_ref = next(it)
    if has_y:
        y_ref, mpp_ref, mps_ref, gpost_ref = next(it), next(it), next(it), next(it)
    if has_h:
        mhp_ref, mhs_ref, gpre_ref = next(it), next(it), next(it)
    if has_y:
        xo_ref = next(it)
    if has_h:
        h_ref = next(it)

    def run(mpost_ref, mpre_ref):
        x = x_ref[...]
        if has_y:
            gate = mpost_ref[:, gate_idx:gate_idx + 1, :]
            x = x + gate * (_rms(y_ref[...]) * gpost_ref[...])
            xo_ref[...] = x
        if has_h:
            shift = mpre_ref[:, pre_idx:pre_idx + 1, :]
            scale = mpre_ref[:, pre_idx + 1:pre_idx + 2, :]
            h_ref[...] = ((_rms(x) * gpre_ref[...]) * (1.0 + scale) + shift).astype(h_ref.dtype)

    i = pl.program_id(0)

    @pl.when(i < nblk_p)
    def _():
        run(mpp_ref if has_y else None, mhp_ref if has_h else None)

    @pl.when(i >= nblk_p)
    def _():
        run(mps_ref if has_y else None, mhs_ref if has_h else None)


def _postpre(x, y, mod_post, g_post, mod_pre, g_pre, *, dims, gate_idx, pre_idx):
    b, t, db, ts = dims
    ng, _, d = x.shape
    gb = max(1, 256 // ts)
    while db % gb or (t // ts) % gb:
        gb //= 2
    bpb = (t // ts) // gb
    nblk_p = b * bpb
    nblk = ng // gb
    has_y, has_h = y is not None, mod_pre is not None

    xspec = pl.BlockSpec((gb, ts, d), lambda i: (i, 0, 0))
    pspec = pl.BlockSpec((1, 6, d), lambda i: (jnp.minimum(i // bpb, b - 1), 0, 0))
    sspec = pl.BlockSpec((gb, 6, d), lambda i: (jnp.maximum(i - nblk_p, 0), 0, 0))
    gspec = pl.BlockSpec((1, 1, d), lambda i: (0, 0, 0))
    args, specs = [x], [xspec]
    if has_y:
        args += [y, mod_post[0], mod_post[1], g_post.reshape(1, 1, d)]
        specs += [xspec, pspec, sspec, gspec]
    if has_h:
        args += [mod_pre[0], mod_pre[1], g_pre.reshape(1, 1, d)]
        specs += [pspec, sspec, gspec]
    out_shape, out_specs = [], []
    if has_y:
        out_shape.append(jax.ShapeDtypeStruct(x.shape, F32))
        out_specs.append(xspec)
    if has_h:
        out_shape.append(jax.ShapeDtypeStruct(x.shape, BF16))
        out_specs.append(xspec)
    outs = pl.pallas_call(
        functools.partial(_postpre_kernel, nblk_p=nblk_p, has_y=has_y, has_h=has_h,
                          gate_idx=gate_idx, pre_idx=pre_idx),
        out_shape=out_shape, grid=(nblk,), in_specs=specs, out_specs=out_specs,
        compiler_params=_cp("arbitrary"), name="postpre",
    )(*args)
    return outs


def _mm_kernel(*refs, nk, prologue, epilogue):
    a_ref, w_ref = refs[0], refs[1]
    idx = 2
    if prologue == "rms":
        g_ref = refs[idx]
        idx += 1
    o_ref = refs[idx]
    acc_ref = refs[idx + 1] if nk > 1 else None

    a = a_ref[...]
    if prologue == "rms":
        a = _rms(a.astype(F32)) * g_ref[...]
    part = _dot(a.astype(BF16), w_ref[...].astype(BF16))

    def finish(v):
        if epilogue == "relu2":
            r = jnp.maximum(v, 0.0)
            v = r * r
        o_ref[...] = v.astype(o_ref.dtype)

    if nk == 1:
        finish(part)
    else:
        k = pl.program_id(2)

        @pl.when(k == 0)
        def _():
            acc_ref[...] = part

        @pl.when(jnp.logical_and(k > 0, k < nk - 1))
        def _():
            acc_ref[...] += part

        @pl.when(k == nk - 1)
        def _():
            finish(acc_ref[...] + part)


def _matmul(a, w, out_dtype, *, bm=1024, bn=512, bk=None, a_cols=None, gain=None, epilogue=None):
    m = a.shape[0]
    k, n = w.shape
    bm = _pick(m, bm)
    bn = min(bn, n)
    bk = k if bk is None else _pick(k, bk)
    nk = k // bk
    col0 = 0
    if a_cols is not None:
        assert a_cols[1] == k and a_cols[0] % bk == 0
        col0 = a_cols[0] // bk
    prologue = None
    args = [a, w]
    specs = [pl.BlockSpec((bm, bk), lambda i, j, kk: (i, col0 + kk)),
             pl.BlockSpec((bk, bn), lambda i, j, kk: (kk, j))]
    if gain is not None:
        assert nk == 1
        prologue = "rms"
        args.append(gain.reshape(1, k))
        specs.append(pl.BlockSpec((1, k), lambda i, j, kk: (0, 0)))
    return pl.pallas_call(
        functools.partial(_mm_kernel, nk=nk, prologue=prologue, epilogue=epilogue),
        out_shape=jax.ShapeDtypeStruct((m, n), out_dtype),
        grid=(m // bm, pl.cdiv(n, bn), nk),
        in_specs=specs,
        out_specs=pl.BlockSpec((bm, bn), lambda i, j, kk: (i, j)),
        scratch_shapes=[pltpu.VMEM((bm, bn), F32)] if nk > 1 else [],
        compiler_params=_cp("arbitrary", "arbitrary", "arbitrary"),
        name="matmul",
    )(*args)


def _rope128(x, cosf, sins):
    lane = lax.broadcasted_iota(jnp.int32, x.shape, 1)
    first = (lane & 63) < 32
    swapped = jnp.where(first, pltpu.roll(x, 96, 1), pltpu.roll(x, 32, 1))
    return x * cosf + swapped * sins


def _qhead_kernel(qn_ref, qr_ref, wuk_ref, cos_ref, sin_ref, ql_ref, qp_ref):
    ql_ref[...] = _dot(qn_ref[...].astype(BF16), wuk_ref[0].astype(BF16)).astype(ql_ref.dtype)
    qp_ref[...] = _rope128(qr_ref[...], cos_ref[...], sin_ref[...]).astype(qp_ref.dtype)


def _qhead(q, wuk_t, cosf, sins, *, heads):
    m = q.shape[0]
    _, nope, rank = wuk_t.shape
    bm = _pick(m, 1024)
    return pl.pallas_call(
        _qhead_kernel,
        out_shape=[jax.ShapeDtypeStruct((m, heads * rank), BF16),
                   jax.ShapeDtypeStruct((m, heads * LANES), BF16)],
        grid=(m // bm, heads),
        in_specs=[pl.BlockSpec((bm, nope), lambda i, h: (i, h)),
                  pl.BlockSpec((bm, LANES), lambda i, h: (i, heads + h)),
                  pl.BlockSpec((1, nope, rank), lambda i, h: (h, 0, 0)),
                  pl.BlockSpec((bm, LANES), lambda i, h: (i, 0)),
                  pl.BlockSpec((bm, LANES), lambda i, h: (i, 0))],
        out_specs=[pl.BlockSpec((bm, rank), lambda i, h: (i, h)),
                   pl.BlockSpec((bm, LANES), lambda i, h: (i, h))],
        compiler_params=_cp("arbitrary", "arbitrary"),
        name="qhead",
    )(q, q, wuk_t, cosf, sins)


def _headmm_kernel(a_ref, w_ref, o_ref):
    o_ref[...] = _dot(a_ref[...].astype(BF16), w_ref[0].astype(BF16)).astype(o_ref.dtype)


def _headmm(a, w, out_dtype):
    m = a.shape[0]
    heads, k, n = w.shape
    bm = _pick(m, 1024)
    return pl.pallas_call(
        _headmm_kernel,
        out_shape=jax.ShapeDtypeStruct((m, heads * n), out_dtype),
        grid=(m // bm, heads),
        in_specs=[pl.BlockSpec((bm, k), lambda i, h: (i, h)),
                  pl.BlockSpec((1, k, n), lambda i, h: (h, 0, 0))],
        out_specs=pl.BlockSpec((bm, n), lambda i, h: (i, h)),
        compiler_params=_cp("arbitrary", "arbitrary"),
        name="headmm",
    )(a, w)


def _kvrows_kernel(z_ref, g_ref, cos_ref, sin_ref, of_ref, ob_ref, *, rank):
    z = z_ref[...]
    lat = _rms(z[:, :rank]) * g_ref[...]
    pe = _rope128(z[:, rank:], cos_ref[...], sin_ref[...])
    of_ref[:, :rank] = lat
    of_ref[:, rank:] = pe
    ob_ref[:, :rank] = lat.astype(BF16)
    ob_ref[:, rank:] = pe.astype(BF16)


def _kvrows(z, col_blk, g_kv, cosf, sins, *, rank):
    m = z.shape[0]
    w = rank + LANES
    bm = _pick(m, 1024)
    return pl.pallas_call(
        functools.partial(_kvrows_kernel, rank=rank),
        out_shape=[jax.ShapeDtypeStruct((m, w), F32), jax.ShapeDtypeStruct((m, w), BF16)],
        grid=(m // bm,),
        in_specs=[pl.BlockSpec((bm, w), lambda i: (i, col_blk)),
                  pl.BlockSpec((1, rank), lambda i: (0, 0)),
                  pl.BlockSpec((bm, LANES), lambda i: (i, 0)),
                  pl.BlockSpec((bm, LANES), lambda i: (i, 0))],
        out_specs=[pl.BlockSpec((bm, w), lambda i: (i, 0)), pl.BlockSpec((bm, w), lambda i: (i, 0))],
        compiler_params=_cp("arbitrary"),
        name="kvrows",
    )(z, g_kv.reshape(1, rank), cosf, sins)


def _softmax_step(s, v, m, l, acc, v_transposed=False):
    m_new = jnp.maximum(m, jnp.max(s, axis=-1, keepdims=True))
    alpha = jnp.exp(m - m_new)
    p = jnp.exp(s - m_new)
    l_new = alpha * l + jnp.sum(p, axis=-1, keepdims=True)
    pb = p.astype(BF16)
    acc_new = alpha * acc + (_dot_nt(pb, v) if v_transposed else _dot(pb, v))
    return m_new, l_new, acc_new


def _softmax_step_ref(s, v, m_ref, l_ref, acc_ref, v_transposed=False):
    m, l, acc = _softmax_step(s, v, m_ref[...], l_ref[...], acc_ref[...], v_transposed)
    m_ref[...] = m
    l_ref[...] = l
    acc_ref[...] = acc


def _lambda_value(lam_ref):
    lv = lam_ref[...]
    d1 = jnp.sum(lv[0:1] * lv[1:2], axis=-1, keepdims=True)
    d2 = jnp.sum(lv[2:3] * lv[3:4], axis=-1, keepdims=True)
    return jnp.exp(d1) - jnp.exp(d2) + DIFF_LAMBDA_INIT


def _split_halves(q):
    lane = lax.broadcasted_iota(jnp.int32, q.shape, 1)
    lo = lane < (q.shape[1] // 2)
    zero = jnp.zeros_like(q)
    return jnp.where(lo, q, zero), jnp.where(lo, zero, q)


def _diff_finish(acc1, l1, acc2, l2, lam, g):
    a = acc1 / l1 - lam * (acc2 / l2)
    return (_rms(a) * g) * (1.0 - DIFF_LAMBDA_INIT)


def _mla_prompt_kernel(ql_ref, qp_ref, k_ref, o_ref, *, tq, rank, scale):
    qi = pl.program_id(1)
    ql = ql_ref[...]
    qp = qp_ref[...]
    row = lax.broadcasted_iota(jnp.int32, (tq, tq), 0)
    col = lax.broadcasted_iota(jnp.int32, (tq, tq), 1)

    def body(c, carry):
        kc = k_ref[pl.ds(pl.multiple_of(c * tq, tq), tq), :]
        lat = kc[:, :rank]
        s = (_dot_nt(ql, lat) + _dot_nt(qp, kc[:, rank:])) * scale
        s = jnp.where(jnp.logical_or(c < qi, col <= row), s, NEG_INF)
        return _softmax_step(s, lat, *carry)

    init = (jnp.full((tq, 1), NEG_INF, F32), jnp.zeros((tq, 1), F32), jnp.zeros((tq, rank), F32))
    _, l, acc = lax.fori_loop(0, qi + 1, body, init)
    o_ref[...] = (acc / l).astype(o_ref.dtype)


def _mla_prompt(q_lat, q_pe, rows_bf, *, b, t, heads, rank, scale):
    tq = _pick(t, 512)
    nq = t // tq
    w = rank + LANES
    return pl.pallas_call(
        functools.partial(_mla_prompt_kernel, tq=tq, rank=rank, scale=scale),
        out_shape=jax.ShapeDtypeStruct((b * t, heads * rank), BF16),
        grid=(b, nq, heads),
        in_specs=[pl.BlockSpec((tq, rank), lambda bb, i, h: (bb * nq + i, h)),
                  pl.BlockSpec((tq, LANES), lambda bb, i, h: (bb * nq + i, h)),
                  pl.BlockSpec((t, w), lambda bb, i, h: (bb, 0))],
        out_specs=pl.BlockSpec((tq, rank), lambda bb, i, h: (bb * nq + i, h)),
        compiler_params=_cp("arbitrary", "arbitrary", "arbitrary"),
        name="mla_prompt",
    )(q_lat, q_pe, rows_bf)


def _diff_prompt_kernel(slopes_ref, q_ref, k_ref, v_ref, lam_ref, g_ref, o_ref, *, tq, dv, scale):
    qi = pl.program_id(1)
    slope = slopes_ref[pl.program_id(2)]
    q1, q2 = _split_halves(q_ref[...].astype(BF16))
    row = lax.broadcasted_iota(jnp.int32, (tq, tq), 0)
    col = lax.broadcasted_iota(jnp.int32, (tq, tq), 1)
    rel = (row - col).astype(F32)

    def body(c, carry):
        off = pl.multiple_of(c * tq, tq)
        k = k_ref[pl.ds(off, tq), :].astype(BF16)
        v = v_ref[pl.ds(off, tq), :].astype(BF16)
        dist = rel + ((qi - c) * tq).astype(F32)
        bias = slope * dist
        keep = dist >= 0.0
        s1 = jnp.where(keep, _dot_nt(q1, k) * scale - bias, NEG_INF)
        s2 = jnp.where(keep, _dot_nt(q2, k) * scale - bias, NEG_INF)
        return _softmax_step(s1, v, *carry[:3]) + _softmax_step(s2, v, *carry[3:])

    one = (jnp.full((tq, 1), NEG_INF, F32), jnp.zeros((tq, 1), F32), jnp.zeros((tq, dv), F32))
    _, l1, acc1, _, l2, acc2 = lax.fori_loop(0, qi + 1, body, one + one)
    o_ref[...] = _diff_finish(acc1, l1, acc2, l2, _lambda_value(lam_ref), g_ref[...]).astype(o_ref.dtype)


def _diff_prompt(slopes, z0, lam_in, g_sub, *, b, t, heads, q_blk, k_blk, v_blk, dv, scale):
    tq = _pick(t, 512)
    nq = t // tq
    return pl.pallas_call(
        functools.partial(_diff_prompt_kernel, tq=tq, dv=dv, scale=scale),
        out_shape=jax.ShapeDtypeStruct((b * t, heads * dv), BF16),
        grid=(b, nq, heads),
        in_specs=[pl.BlockSpec(memory_space=pltpu.SMEM),
                  pl.BlockSpec((tq, LANES), lambda bb, i, h: (bb * nq + i, q_blk + h)),
                  pl.BlockSpec((t, LANES), lambda bb, i, h: (bb, k_blk)),
                  pl.BlockSpec((t, dv), lambda bb, i, h: (bb, v_blk)),
                  pl.BlockSpec((8, LANES), lambda bb, i, h: (0, 0)),
                  pl.BlockSpec((1, dv), lambda bb, i, h: (0, 0))],
        out_specs=pl.BlockSpec((tq, dv), lambda bb, i, h: (bb * nq + i, h)),
        compiler_params=_cp("arbitrary", "arbitrary", "arbitrary"),
        name="diff_prompt",
    )(slopes, z0, z0, z0, lam_in, g_sub.reshape(1, dv))


def _ab_sample_kernel(pt_ref, ql_ref, qp_ref, qd_ref, kn_ref, dkn_ref, dvn_ref, slope_ref, t_ref,
                      lam_ref, g_ref, *rest, npg, page, rank, rope, past, nchunk, scale_a, scale_d):
    mla_pg = rest[:npg]
    dk_pg = rest[npg:2 * npg]
    dv_pg = rest[2 * npg:3 * npg]
    ol_ref, od_ref = rest[3 * npg:3 * npg + 2]
    (kt, dkb, dvb, m_a, l_a, acc_a, m_1, l_1, acc_1, m_2, l_2, acc_2) = rest[3 * npg + 2:]
    j = pl.program_id(1)
    kc = npg * page
    row_w = rank + rope

    @pl.when(j == 0)
    def _():
        for m_ref, l_ref, acc_ref in ((m_a, l_a, acc_a), (m_1, l_1, acc_1), (m_2, l_2, acc_2)):
            m_ref[...] = jnp.full(m_ref.shape, NEG_INF, F32)
            l_ref[...] = jnp.zeros(l_ref.shape, F32)
            acc_ref[...] = jnp.zeros(acc_ref.shape, F32)
        kt[row_w:, :] = jnp.zeros((kt.shape[0] - row_w, kc), BF16)

    for p in range(npg):
        cols = slice(p * page, (p + 1) * page)
        kt[:row_w, cols] = mla_pg[p][0].astype(BF16)
        dkb[cols, :] = dk_pg[p][0].astype(BF16)
        dvb[cols, :] = dv_pg[p][0].astype(BF16)

    ql = ql_ref[0]
    qp = qp_ref[0]
    q1, q2 = _split_halves(qd_ref[0].astype(BF16))
    slope = slope_ref[...]
    tq = t_ref[...]
    r = ql.shape[0]

    lat_t = kt[:rank, :]
    s = (_dot(ql, lat_t) + _dot(qp, kt[rank:, :])) * scale_a
    _softmax_step_ref(s, lat_t, m_a, l_a, acc_a, v_transposed=True)

    kpos = (j * kc + lax.broadcasted_iota(jnp.int32, (r, kc), 1)).astype(F32)
    bias = slope * ((tq + float(past)) - kpos)
    kd = dkb[...]
    vd = dvb[...]
    _softmax_step_ref(_dot_nt(q1, kd) * scale_d - bias, vd, m_1, l_1, acc_1)
    _softmax_step_ref(_dot_nt(q2, kd) * scale_d - bias, vd, m_2, l_2, acc_2)

    @pl.when(j == nchunk - 1)
    def _():
        kn = kn_ref[0]
        nk = kn.shape[0]
        dist = tq - lax.broadcasted_iota(jnp.int32, (r, nk), 1).astype(F32)
        keep = dist >= 0.0
        latn = kn[:, :rank]
        sn = (_dot_nt(ql, latn) + _dot_nt(qp, kn[:, rank:])) * scale_a
        _softmax_step_ref(jnp.where(keep, sn, NEG_INF), latn, m_a, l_a, acc_a)
        biasn = slope * dist
        dkn = dkn_ref[0]
        dvn = dvn_ref[0]
        _softmax_step_ref(jnp.where(keep, _dot_nt(q1, dkn) * scale_d - biasn, NEG_INF), dvn, m_1, l_1, acc_1)
        _softmax_step_ref(jnp.where(keep, _dot_nt(q2, dkn) * scale_d - biasn, NEG_INF), dvn, m_2, l_2, acc_2)
        ol_ref[0] = (acc_a[...] / l_a[...]).astype(ol_ref.dtype)
        od_ref[0] = _diff_finish(acc_1[...], l_1[...], acc_2[...], l_2[...],
                                 _lambda_value(lam_ref), g_ref[...]).astype(od_ref.dtype)


def _pages_per_step(n_pages):
    for p in (16, 8, 4, 2, 1):
        if n_pages % p == 0 and n_pages // p >= 2:
            return p
    return 1


def _ab_sample(pt_flat, ql, qp, qd, kn, dkn, dvn, slope_col, t_col, lam_in, g_sub,
               cache_mla, cache_dk, cache_dv, *, n_pages, rank, rope, scale_a, scale_d):
    db, r, _ = ql.shape
    row_w, page = cache_mla.shape[1:]
    dv = cache_dv.shape[2]
    npg = _pages_per_step(n_pages)
    nchunk = n_pages // npg
    kc = npg * page
    past = n_pages * page

    def page_spec(shape, p):
        return pl.BlockSpec((1,) + shape, lambda b, j, pt: (pt[b * n_pages + j * npg + p], 0, 0))

    def seq_spec(shape):
        return pl.BlockSpec((1,) + shape, lambda b, j, pt: (b, 0, 0))

    def const_spec(shape):
        return pl.BlockSpec(shape, lambda b, j, pt: (0, 0))

    in_specs = [seq_spec((r, rank)), seq_spec((r, LANES)), seq_spec((r, LANES)),
                seq_spec((NEW_KEY_PAD, rank + LANES)), seq_spec((NEW_KEY_PAD, LANES)),
                seq_spec((NEW_KEY_PAD, dv)),
                const_spec((r, 1)), const_spec((r, 1)), const_spec((8, LANES)), const_spec((1, dv))]
    in_specs += [page_spec((row_w, page), p) for p in range(npg)]
    in_specs += [page_spec((page, LANES), p) for p in range(npg)]
    in_specs += [page_spec((page, dv), p) for p in range(npg)]
    stat = lambda w: pltpu.VMEM((r, w), F32)
    scratch = [pltpu.VMEM((rank + LANES, kc), BF16),
               pltpu.VMEM((kc, LANES), BF16), pltpu.VMEM((kc, dv), BF16),
               stat(1), stat(1), stat(rank), stat(1), stat(1), stat(dv), stat(1), stat(1), stat(dv)]
    return pl.pallas_call(
        functools.partial(_ab_sample_kernel, npg=npg, page=page, rank=rank, rope=rope, past=past,
                          nchunk=nchunk, scale_a=scale_a, scale_d=scale_d),
        out_shape=[jax.ShapeDtypeStruct((db, r, rank), BF16), jax.ShapeDtypeStruct((db, r, dv), BF16)],
        grid_spec=pltpu.PrefetchScalarGridSpec(
            num_scalar_prefetch=1, grid=(db, nchunk), in_specs=in_specs,
            out_specs=[seq_spec((r, rank)), seq_spec((r, dv))], scratch_shapes=scratch),
        compiler_params=_cp("arbitrary", "arbitrary"),
        name="ab_sample",
    )(pt_flat, ql, qp, qd, kn, dkn, dvn, slope_col, t_col, lam_in, g_sub.reshape(1, dv),
      *([cache_mla] * npg), *([cache_dk] * npg), *([cache_dv] * npg))


def _topk_bias(gate, n_valid, n_sel, count):
    lane = lax.broadcasted_iota(jnp.int32, gate.shape, 1)

    def body(k, rank):
        gk = jnp.sum(jnp.where(lane == k, gate, 0.0), axis=-1, keepdims=True)
        beats = jnp.logical_or(gk > gate, jnp.logical_and(gk == gate, k < lane))
        return rank + jnp.where(jnp.logical_and(beats, k < n_valid), 1.0, 0.0)

    rank = lax.fori_loop(0, count, body, jnp.zeros(gate.shape, F32))
    return jnp.where(jnp.logical_and(lane < n_valid, rank < float(n_sel)), 0.0, NEG_INF)


def _lane_pick(x, k):
    lane = lax.broadcasted_iota(jnp.int32, x.shape, 1)
    return jnp.sum(jnp.where(lane == k, x, 0.0), axis=-1, keepdims=True)


def _moba_prompt_kernel(slopes_ref, q_ref, k_ref, v_ref, o_ref, kb, vb, kmean, *, blk, nblk, rh, scale):
    g = pl.program_id(1)
    qi = pl.program_id(2)
    rr = pl.program_id(3)
    slope = slopes_ref[g * rh + rr]
    d = q_ref.shape[1]

    @pl.when(jnp.logical_and(qi == 0, rr == 0))
    def _():
        kb[...] = k_ref[...].astype(BF16)
        vb[...] = v_ref[...].astype(BF16)
        kmean[...] = jnp.zeros(kmean.shape, F32)
        for n in range(nblk):
            kmean[n:n + 1, :] = jnp.mean(k_ref[n * blk:(n + 1) * blk, :], axis=0, keepdims=True)

    q = q_ref[...]
    qb = q.astype(BF16)
    bias_all = _topk_bias(_dot3_nt(q, kmean[...]), qi, min(MOBA_TOPK, nblk - 1), nblk)

    row = lax.broadcasted_iota(jnp.int32, (blk, blk), 0)
    col = lax.broadcasted_iota(jnp.int32, (blk, blk), 1)
    rel = (row - col).astype(F32)

    own = pl.multiple_of(qi * blk, blk)
    s = jnp.where(rel >= 0.0, _dot_nt(qb, kb[pl.ds(own, blk), :]) * scale - slope * rel, NEG_INF)
    init = _softmax_step(s, vb[pl.ds(own, blk), :], jnp.full((blk, 1), NEG_INF, F32),
                         jnp.zeros((blk, 1), F32), jnp.zeros((blk, d), F32))

    def body(jb, carry):
        off = pl.multiple_of(jb * blk, blk)
        dist = rel + ((qi - jb) * blk).astype(F32)
        sj = _dot_nt(qb, kb[pl.ds(off, blk), :]) * scale - slope * dist + _lane_pick(bias_all, jb)
        return _softmax_step(sj, vb[pl.ds(off, blk), :], *carry)

    _, l, acc = lax.fori_loop(0, qi, body, init)
    o_ref[...] = (acc / l).astype(o_ref.dtype)


def _moba_prompt(slopes, zc, *, b, t, heads, groups, d, q_blk, k_blk, v_blk, scale):
    blk = MOBA_BLOCK
    assert t % blk == 0
    nblk = t // blk
    rh = heads // groups
    return pl.pallas_call(
        functools.partial(_moba_prompt_kernel, blk=blk, nblk=nblk, rh=rh, scale=scale),
        out_shape=jax.ShapeDtypeStruct((b * t, heads * d), BF16),
        grid=(b, groups, nblk, rh),
        in_specs=[pl.BlockSpec(memory_space=pltpu.SMEM),
                  pl.BlockSpec((blk, d), lambda bb, g, i, r: (bb * nblk + i, q_blk + g * rh + r)),
                  pl.BlockSpec((t, d), lambda bb, g, i, r: (bb, k_blk + g)),
                  pl.BlockSpec((t, d), lambda bb, g, i, r: (bb, v_blk + g))],
        out_specs=pl.BlockSpec((blk, d), lambda bb, g, i, r: (bb * nblk + i, g * rh + r)),
        scratch_shapes=[pltpu.VMEM((t, d), BF16), pltpu.VMEM((t, d), BF16), pltpu.VMEM((LANES, d), F32)],
        compiler_params=_cp("arbitrary", "arbitrary", "arbitrary", "arbitrary"),
        name="moba_prompt",
    )(slopes, zc, zc, zc)


def _moba_sample_kernel(pt_ref, q_ref, kn_ref, vn_ref, slope_ref, t_ref, *rest,
                        npg, page, groups, d, past, nchunk, nfull, scale):
    k_pg = rest[:npg]
    v_pg = rest[npg:2 * npg]
    o_ref = rest[2 * npg]
    s_all, kb, vb, ksum, sel, m_s, l_s, acc_s = rest[2 * npg + 1:]
    j = pl.program_id(1)
    kc = npg * page
    ppb = MOBA_BLOCK // page
    bpc = kc // MOBA_BLOCK
    r = q_ref.shape[2]
    tq = t_ref[...]

    @pl.when(j == 0)
    def _():
        ksum[...] = jnp.zeros(ksum.shape, F32)

    @pl.when(j < nchunk)
    def _():
        for g in range(groups):
            csum = None
            for p in range(npg):
                x = k_pg[p][0, pl.ds(g, page, stride=groups), :]
                kb[p * page:(p + 1) * page, g * d:(g + 1) * d] = x.astype(BF16)
                cs = jnp.sum(x, axis=0, keepdims=True)
                csum = cs if p % ppb == 0 else csum + cs
                if p % ppb == ppb - 1:
                    ksum[g, pl.ds(j * bpc + p // ppb, 1), :] = csum
        kpos = (j * kc + lax.broadcasted_iota(jnp.int32, (r, kc), 1)).astype(F32)
        dist = (tq + float(past)) - kpos
        for g in range(groups):
            qb = q_ref[0, g].astype(BF16)
            s_all[g, j] = _dot_nt(qb, kb[:, g * d:(g + 1) * d]) * scale - slope_ref[g] * dist

    @pl.when(j == nchunk - 1)
    def _():
        kn = kn_ref[0]
        vn = vn_ref[0]
        nk = kn.shape[0]
        distn = tq - lax.broadcasted_iota(jnp.int32, (r, nk), 1).astype(F32)
        for g in range(groups):
            q = q_ref[0, g]
            kmean = ksum[g] * (1.0 / MOBA_BLOCK)
            sel[g] = _topk_bias(_dot3_nt(q, kmean), nfull, min(MOBA_TOPK, nfull), nfull)
            sn = _dot_nt(q.astype(BF16), kn[:, g * d:(g + 1) * d]) * scale - slope_ref[g] * distn
            m, l, acc = _softmax_step(jnp.where(distn >= 0.0, sn, NEG_INF), vn[:, g * d:(g + 1) * d],
                                      jnp.full((r, 1), NEG_INF, F32), jnp.zeros((r, 1), F32),
                                      jnp.zeros((r, d), F32))
            m_s[g] = m
            l_s[g] = l
            acc_s[g] = acc

    @pl.when(j >= nchunk)
    def _():
        c = j - nchunk
        for g in range(groups):
            for p in range(npg):
                vb[p * page:(p + 1) * page, g * d:(g + 1) * d] = (
                    v_pg[p][0, pl.ds(g, page, stride=groups), :].astype(BF16))
        for g in range(groups):
            s = s_all[g, c]
            bias_all = sel[g]
            s = jnp.concatenate(
                [s[:, n * MOBA_BLOCK:(n + 1) * MOBA_BLOCK] + _lane_pick(bias_all, c * bpc + n)
                 for n in range(bpc)], axis=1)
            m, l, acc = _softmax_step(s, vb[:, g * d:(g + 1) * d], m_s[g], l_s[g], acc_s[g])
            m_s[g] = m
            l_s[g] = l
            acc_s[g] = acc

    @pl.when(j == 2 * nchunk - 1)
    def _():
        for g in range(groups):
            o_ref[0, g] = (acc_s[g] / l_s[g]).astype(o_ref.dtype)


def _moba_sample(pt_flat, q, kn, vn, slope_col, t_col, cache_k, cache_v, *, n_pages, groups, d, scale):
    db, _, r, _ = q.shape
    page = cache_k.shape[1] // groups
    assert MOBA_BLOCK % page == 0
    past = n_pages * page
    assert past % MOBA_BLOCK == 0
    nfull = past // MOBA_BLOCK
    assert 1 <= nfull <= LANES
    npg = _pages_per_step(n_pages)
    while (npg * page) % MOBA_BLOCK:
        npg *= 2
    nchunk = n_pages // npg
    kc = npg * page
    gd = groups * d

    def kspec(p):
        return pl.BlockSpec((1, page * groups, d), lambda b, j, pt:
                            (pt[b * n_pages + jnp.minimum(j, nchunk - 1) * npg + p], 0, 0))

    def vspec(p):
        return pl.BlockSpec((1, page * groups, d), lambda b, j, pt:
                            (pt[b * n_pages + jnp.maximum(j - nchunk, 0) * npg + p], 0, 0))

    in_specs = [pl.BlockSpec((1, groups, r, d), lambda b, j, pt: (b, 0, 0, 0)),
                pl.BlockSpec((1, NEW_KEY_PAD, gd), lambda b, j, pt: (b, 0, 0)),
                pl.BlockSpec((1, NEW_KEY_PAD, gd), lambda b, j, pt: (b, 0, 0)),
                pl.BlockSpec((groups, r, 1), lambda b, j, pt: (0, 0, 0)),
                pl.BlockSpec((r, 1), lambda b, j, pt: (0, 0))]
    in_specs += [kspec(p) for p in range(npg)] + [vspec(p) for p in range(npg)]
    scratch = [pltpu.VMEM((groups, nchunk, r, kc), F32),
               pltpu.VMEM((kc, gd), BF16), pltpu.VMEM((kc, gd), BF16),
               pltpu.VMEM((groups, LANES, d), F32), pltpu.VMEM((groups, r, LANES), F32),
               pltpu.VMEM((groups, r, 1), F32), pltpu.VMEM((groups, r, 1), F32),
               pltpu.VMEM((groups, r, d), F32)]
    return pl.pallas_call(
        functools.partial(_moba_sample_kernel, npg=npg, page=page, groups=groups, d=d, past=past,
                          nchunk=nchunk, nfull=nfull, scale=scale),
        out_shape=jax.ShapeDtypeStruct((db, groups, r, d), BF16),
        grid_spec=pltpu.PrefetchScalarGridSpec(
            num_scalar_prefetch=1, grid=(db, 2 * nchunk), in_specs=in_specs,
            out_specs=pl.BlockSpec((1, groups, r, d), lambda b, j, pt: (b, 0, 0, 0)),
            scratch_shapes=scratch),
        compiler_params=_cp("arbitrary", "arbitrary"),
        name="moba_sample",
    )(pt_flat, q, kn, vn, slope_col, t_col, *([cache_k] * npg), *([cache_v] * npg))


def _alibi_slopes(n):
    return np.asarray(2.0 ** (-8.0 * np.arange(1, n + 1) / n), dtype=np.float32)


def _rope_tables(pos, half):
    inv = ROPE_THETA ** (-jnp.arange(half, dtype=F32) / half)
    ang = pos.astype(F32)[:, None] * inv
    cos, sin = jnp.cos(ang), jnp.sin(ang)
    reps = LANES // (2 * half)
    return jnp.tile(cos, (1, 2 * reps)), jnp.tile(jnp.concatenate([-sin, sin], axis=-1), (1, reps))


def _pad_new(x, db, ts):
    x = x.reshape(db, ts, x.shape[-1]).astype(BF16)
    return jnp.pad(x, ((0, 0), (0, NEW_KEY_PAD - ts), (0, 0)))


def kernel(x_prompt, x_sample, cache_mla, cache_diff_k, cache_diff_v, cache_moba_k, cache_moba_v,
           page_table, c_prompt, c_sample, w_ada, b_ada, g_pre_mix, g_post_mix, g_pre_ffn, g_post_ffn,
           w_in_ab, g_q_a, w_q_b, g_kv_a, w_uk, w_uv, lambda_q1, lambda_k1, lambda_q2, lambda_k2,
           g_diff_sub, w_out_ab, w_in_c, w_out_c, w_ff_up, w_ff_down):
    b, t, d_model = x_prompt.shape
    db, ts, _ = x_sample.shape
    mp, ms = b * t, db * ts
    m = mp + ms
    dims = (b, t, db, ts)
    n_pool, page, row_w = cache_mla.shape
    n_pages = page_table.shape[1]
    past = n_pages * page
    q_rank = g_q_a.shape[0]
    kv_rank, heads_a, nope = w_uk.shape
    v_a = w_uv.shape[2]
    rope = row_w - kv_rank
    dh = lambda_q1.shape[0]
    dv = g_diff_sub.shape[0]
    heads_b = (w_in_ab.shape[1] - q_rank - row_w - 2 * dh - dv) // (2 * dh)
    groups, d_c = cache_moba_k.shape[2], cache_moba_k.shape[3]
    heads_c = w_out_c.shape[0] // d_c
    rh = heads_c // groups
    assert nope == LANES and 2 * rope == LANES and 2 * dh == LANES and dv == LANES and d_c == LANES
    assert kv_rank % LANES == 0 and heads_a == heads_b and cache_diff_k.shape[2] == 1 and ts <= 8
    assert ts <= NEW_KEY_PAD

    o1, o2 = q_rank, q_rank + row_w
    o3 = o2 + heads_b * 2 * dh
    w_kv = jnp.pad(w_in_ab[:, o1:o2], ((0, 0), (0, LANES - rope)))
    w0 = jnp.concatenate([w_in_ab[:, :o1], w_in_ab[:, o2:o3], w_kv, w_in_ab[:, o3:]], axis=1)
    c_dq = q_rank
    c_kv = c_dq + heads_b * 2 * dh
    c_dk = c_kv + kv_rank + LANES
    c_dv = c_dk + 2 * dh
    assert c_kv % (kv_rank + LANES) == 0
    wq = w_q_b.reshape(q_rank, heads_a, nope + rope)
    wq_rope = jnp.pad(wq[:, :, nope:], ((0, 0), (0, 0), (0, LANES - rope)))
    w_q = jnp.concatenate([wq[:, :, :nope].reshape(q_rank, heads_a * nope),
                           wq_rope.reshape(q_rank, heads_a * LANES)], axis=1)
    wuk_t = jnp.transpose(w_uk, (1, 2, 0))
    wuv_t = jnp.transpose(w_uv, (1, 0, 2))
    pos_all = jnp.concatenate([jnp.tile(jnp.arange(t, dtype=jnp.int32), b),
                               jnp.tile(past + jnp.arange(ts, dtype=jnp.int32), db)])
    cosf, sins = _rope_tables(pos_all, rope // 2)
    lam_in = jnp.zeros((8, LANES), F32)
    for i, v in enumerate((lambda_q1, lambda_k1, lambda_q2, lambda_k2)):
        lam_in = lam_in.at[i, :dh].set(v.astype(F32))
    slopes_b = _alibi_slopes(heads_b)
    slopes_c = _alibi_slopes(heads_c)
    t_col = jnp.asarray(np.repeat(np.arange(ts, dtype=np.float32), heads_a).reshape(ts * heads_a, 1))
    slope_col_b = jnp.asarray(np.tile(slopes_b, ts).reshape(ts * heads_b, 1))
    t_col_c = jnp.asarray(np.repeat(np.arange(ts, dtype=np.float32), rh).reshape(ts * rh, 1))
    slope_col_c = jnp.asarray(np.tile(slopes_c.reshape(groups, 1, rh), (1, ts, 1)).reshape(groups, ts * rh, 1))
    pt_flat = page_table.reshape(-1)
    cache_dk = cache_diff_k.reshape(n_pool, page, 2 * dh)
    cache_dv = cache_diff_v.reshape(n_pool, page, dv)
    cache_mk = cache_moba_k.reshape(n_pool, page * groups, d_c)
    cache_mv = cache_moba_v.reshape(n_pool, page * groups, d_c)
    cache_mla_t = jnp.swapaxes(cache_mla, 1, 2)

    n_c = b + db
    c_all = jnp.pad(jnp.concatenate([c_prompt, c_sample], axis=0), ((0, -n_c % 16), (0, 0)))
    mod = _ada(c_all, w_ada, b_ada).reshape(w_ada.shape[0], c_all.shape[0], 6, d_model)
    mods = [(mod[l, :b], mod[l, b:n_c]) for l in range(mod.shape[0])]

    x = jnp.concatenate([x_prompt.reshape(mp, d_model), x_sample.reshape(ms, d_model)], axis=0)
    x = x.reshape(m // ts, ts, d_model)
    pp = functools.partial(_postpre, dims=dims)

    def ffn(x, h, l, mod_next, g_next):
        a = _matmul(h.reshape(m, d_model), w_ff_up[l], BF16, bm=1024, bn=512, epilogue="relu2")
        y = _matmul(a, w_ff_down[l], F32, bm=1024, bn=1024, bk=2048)
        return pp(x, y.reshape(x.shape), mods[l], g_post_ffn[l], mod_next, g_next, gate_idx=5, pre_idx=0)

    (h,) = pp(x, None, None, None, mods[0], g_pre_mix[0], gate_idx=2, pre_idx=0)
    z0 = _matmul(h.reshape(m, d_model), w0, F32)
    q = _matmul(z0, w_q, F32, a_cols=(0, q_rank), gain=g_q_a)
    q_lat, q_pe = _qhead(q, wuk_t, cosf, sins, heads=heads_a)
    rows_f, rows_bf = _kvrows(z0, c_kv // (kv_rank + LANES), g_kv_a, cosf, sins, rank=kv_rank)

    scale_a = (nope + rope) ** -0.5
    scale_d = dh ** -0.5
    ol_p = _mla_prompt(q_lat, q_pe, rows_bf, b=b, t=t, heads=heads_a, rank=kv_rank, scale=scale_a)
    od_p = _diff_prompt(jnp.asarray(slopes_b), z0, lam_in, g_diff_sub, b=b, t=t, heads=heads_b,
                        q_blk=c_dq // LANES, k_blk=c_dk // LANES, v_blk=c_dv // LANES, dv=dv, scale=scale_d)
    r_ab = ts * heads_a
    ol_s, od_s = _ab_sample(
        pt_flat,
        q_lat[mp:].reshape(db, r_ab, kv_rank), q_pe[mp:].reshape(db, r_ab, LANES),
        z0[mp:, c_dq:c_kv].reshape(db, r_ab, 2 * dh),
        _pad_new(rows_bf[mp:], db, ts), _pad_new(z0[mp:, c_dk:c_dv], db, ts), _pad_new(z0[mp:, c_dv:], db, ts),
        slope_col_b, t_col, lam_in, g_diff_sub, cache_mla_t, cache_dk, cache_dv,
        n_pages=n_pages, rank=kv_rank, rope=rope, scale_a=scale_a, scale_d=scale_d)
    o_lat = jnp.concatenate([ol_p, ol_s.reshape(ms, heads_a * kv_rank)], axis=0)
    o_d = jnp.concatenate([od_p, od_s.reshape(ms, heads_b * dv)], axis=0)
    o_cat = jnp.concatenate([_headmm(o_lat, wuv_t, BF16), o_d], axis=1)
    mix = _matmul(o_cat, w_out_ab, F32, bm=1024, bn=512)
    x, h = pp(x, mix.reshape(x.shape), mods[0], g_post_mix[0], mods[0], g_pre_ffn[0], gate_idx=2, pre_idx=3)
    x, h = ffn(x, h, 0, mods[1], g_pre_mix[1])

    zc = _matmul(h.reshape(m, d_model), w_in_c, F32)
    nq_c = heads_c * d_c
    scale_c = d_c ** -0.5
    oc_p = _moba_prompt(jnp.asarray(slopes_c), zc, b=b, t=t, heads=heads_c, groups=groups, d=d_c,
                        q_blk=0, k_blk=heads_c, v_blk=heads_c + groups, scale=scale_c)
    q_s = zc[mp:, :nq_c].reshape(db, ts, groups, rh, d_c).transpose(0, 2, 1, 3, 4).reshape(db, groups, ts * rh, d_c)
    oc_s = _moba_sample(pt_flat, q_s, _pad_new(zc[mp:, nq_c:nq_c + groups * d_c], db, ts),
                        _pad_new(zc[mp:, nq_c + groups * d_c:], db, ts), slope_col_c, t_col_c,
                        cache_mk, cache_mv, n_pages=n_pages, groups=groups, d=d_c, scale=scale_c)
    oc_s = oc_s.reshape(db, groups, ts, rh, d_c).transpose(0, 2, 1, 3, 4).reshape(ms, nq_c)
    mix = _matmul(jnp.concatenate([oc_p, oc_s], axis=0), w_out_c, F32, bm=1024, bn=512)
    x, h = pp(x, mix.reshape(x.shape), mods[1], g_post_mix[1], mods[1], g_pre_ffn[1], gate_idx=2, pre_idx=3)
    (x,) = ffn(x, h, 1, None, None)

    x = x.reshape(m, d_model)
    kd = groups * d_c
    return (x[:mp].reshape(b, t, d_model), x[mp:].reshape(db, ts, d_model),
            rows_f[:mp, :row_w].reshape(b, t, row_w),
            z0[:mp, c_dk:c_dv].reshape(b, t, 1, 2 * dh), z0[:mp, c_dv:].reshape(b, t, 1, dv),
            zc[:mp, nq_c:nq_c + kd].reshape(b, t, groups, d_c), zc[:mp, nq_c + kd:].reshape(b, t, groups, d_c),
            rows_f[mp:, :row_w].reshape(db, ts, row_w),
            z0[mp:, c_dk:c_dv].reshape(db, ts, 1, 2 * dh), z0[mp:, c_dv:].reshape(db, ts, 1, dv),
            zc[mp:, nq_c:nq_c + kd].reshape(db, ts, groups, d_c), zc[mp:, nq_c + kd:].reshape(db, ts, groups, d_c))
```

```python
import functools
import math

import numpy as np
import jax
import jax.numpy as jnp
from jax import lax
from jax.experimental import pallas as pl
from jax.experimental.pallas import tpu as pltpu

F32 = jnp.float32
BF16 = jnp.bfloat16
EPS = 1e-6
ROPE_THETA = 10000.0
MOBA_BLOCK = 256
BLOCK_SHIFT = MOBA_BLOCK.bit_length() - 1
MOBA_TOPK = 3
DIFF_LAMBDA_INIT = 0.8 - 0.6 * math.exp(-0.3 * 0)
LOG2E = math.log2(math.e)
NEG_INF = float("-inf")
MASKED = -1e30
LANES = 128
SUBLANES = 8
V7X_VMEM_BYTES = 64 * 1024 * 1024
VMEM_LIMIT = V7X_VMEM_BYTES - 8 * 1024 * 1024
NEW_KEY_PAD = 128
Q_TILE = 256
AB_PAGES_PER_STEP = 32
MOBA_PAGES_PER_STEP = 32


def _cp(*sem):
    return pltpu.CompilerParams(dimension_semantics=sem, vmem_limit_bytes=VMEM_LIMIT)


def _pick(n, pref):
    if n <= pref:
        return n
    d = pref - pref % LANES
    while d >= LANES:
        if n % d == 0:
            return d
        d -= LANES
    return n


def _dot(a, b):
    return jnp.dot(a, b, preferred_element_type=F32)


def _dot_nt(a, b):
    return lax.dot_general(a, b, (((1,), (1,)), ((), ())), preferred_element_type=F32)


def _dot3_nt(a, b):
    ah = a.astype(BF16)
    al = (a - ah.astype(F32)).astype(BF16)
    bh = b.astype(BF16)
    bl = (b - bh.astype(F32)).astype(BF16)
    return _dot_nt(ah, bh) + (_dot_nt(ah, bl) + _dot_nt(al, bh))


def _rms(x):
    return x * lax.rsqrt(jnp.mean(x * x, axis=-1, keepdims=True) + EPS)


def _ada_kernel(c_ref, w_ref, b_ref, o_ref):
    c = c_ref[...]
    a = (c * (1.0 / (1.0 + jnp.exp(-c)))).astype(BF16)
    o_ref[0] = _dot(a, w_ref[0].astype(BF16)) + b_ref[0]


def _ada(c_pad, w_ada, b_ada):
    nl, d, n = w_ada.shape
    bc = c_pad.shape[0]
    bn = _pick(n, 512)
    return pl.pallas_call(
        _ada_kernel,
        out_shape=jax.ShapeDtypeStruct((nl, bc, n), F32),
        grid=(nl, n // bn),
        in_specs=[pl.BlockSpec((bc, d), lambda l, j: (0, 0)),
                  pl.BlockSpec((1, d, bn), lambda l, j: (l, 0, j)),
                  pl.BlockSpec((1, 1, bn), lambda l, j: (l, 0, j))],
        out_specs=pl.BlockSpec((1, bc, bn), lambda l, j: (l, 0, j)),
        compiler_params=_cp("arbitrary", "arbitrary"),
        name="ada",
    )(c_pad, w_ada, b_ada.reshape(nl, 1, n))


def _postpre_kernel(*refs, first_sample_blk, has_y, has_h, gate_idx, pre_idx):
    it = iter(refs)
    x_ref = next(it)
    if has_y:
        y_ref, mpp_ref, mps_ref, gpost_ref = next(it), next(it), next(it), next(it)
    if has_h:
        mhp_ref, mhs_ref, gpre_ref = next(it), next(it), next(it)
    if has_y:
        xo_ref = next(it)
    if has_h:
        h_ref = next(it)

    def run(mpost_ref, mpre_ref):
        x = x_ref[...]
        if has_y:
            gate = mpost_ref[:, gate_idx:gate_idx + 1, :]
            x = x + gate * (_rms(y_ref[...]) * gpost_ref[...])
            xo_ref[...] = x
        if has_h:
            shift = mpre_ref[:, pre_idx:pre_idx + 1, :]
            scale = mpre_ref[:, pre_idx + 1:pre_idx + 2, :]
            h_ref[...] = ((_rms(x) * gpre_ref[...]) * (1.0 + scale) + shift).astype(h_ref.dtype)

    i = pl.program_id(0)

    @pl.when(i < first_sample_blk)
    def _():
        run(mpp_ref if has_y else None, mhp_ref if has_h else None)

    @pl.when(i >= first_sample_blk)
    def _():
        run(mps_ref if has_y else None, mhs_ref if has_h else None)


def _postpre(x, y, mod_post, g_post, mod_pre, g_pre, *, dims, gate_idx, pre_idx, part="all"):
    b, t, db, ts = dims
    ng, _, d = x.shape
    gb = max(1, 256 // ts)
    while db % gb or (t // ts) % gb:
        gb //= 2
    bpb = (t // ts) // gb
    nblk_p = b * bpb
    nblk = ng // gb
    lo, hi = {"all": (0, nblk), "prompt": (0, nblk_p), "sample": (nblk_p, nblk)}[part]
    has_y, has_h = y is not None, mod_pre is not None

    xspec = pl.BlockSpec((gb, ts, d), lambda i: (i + lo, 0, 0))
    ospec = pl.BlockSpec((gb, ts, d), lambda i: (i, 0, 0))
    pspec = pl.BlockSpec((1, 6, d), lambda i: (jnp.minimum((i + lo) // bpb, b - 1), 0, 0))
    sspec = pl.BlockSpec((gb, 6, d), lambda i: (jnp.maximum(i + lo - nblk_p, 0), 0, 0))
    gspec = pl.BlockSpec((1, 1, d), lambda i: (0, 0, 0))
    args, specs = [x], [xspec]
    if has_y:
        args += [y, mod_post[0], mod_post[1], g_post.reshape(1, 1, d)]
        specs += [xspec, pspec, sspec, gspec]
    if has_h:
        args += [mod_pre[0], mod_pre[1], g_pre.reshape(1, 1, d)]
        specs += [pspec, sspec, gspec]
    oshape = ((hi - lo) * gb, ts, d)
    out_shape, out_specs = [], []
    if has_y:
        out_shape.append(jax.ShapeDtypeStruct(oshape, F32))
        out_specs.append(ospec)
    if has_h:
        out_shape.append(jax.ShapeDtypeStruct(oshape, BF16))
        out_specs.append(ospec)
    outs = pl.pallas_call(
        functools.partial(_postpre_kernel, first_sample_blk=nblk_p - lo, has_y=has_y, has_h=has_h,
                          gate_idx=gate_idx, pre_idx=pre_idx),
        out_shape=out_shape, grid=(hi - lo,), in_specs=specs, out_specs=out_specs,
        compiler_params=_cp("arbitrary"), name="postpre",
    )(*args)
    return outs


def _mm_kernel(*refs, nk, prologue, epilogue):
    a_ref, w_ref = refs[0], refs[1]
    idx = 2
    if prologue == "rms":
        g_ref = refs[idx]
        idx += 1
    o_ref = refs[idx]
    acc_ref = refs[idx + 1] if nk > 1 else None

    a = a_ref[...]
    if prologue == "rms":
        a = _rms(a.astype(F32)) * g_ref[...]
    part = _dot(a.astype(BF16), w_ref[...].astype(BF16))

    def finish(v):
        if epilogue == "relu2":
            r = jnp.maximum(v, 0.0)
            v = r * r
        o_ref[...] = v.astype(o_ref.dtype)

    if nk == 1:
        finish(part)
    else:
        k = pl.program_id(2)

        @pl.when(k == 0)
        def _():
            acc_ref[...] = part

        @pl.when(jnp.logical_and(k > 0, k < nk - 1))
        def _():
            acc_ref[...] += part

        @pl.when(k == nk - 1)
        def _():
            finish(acc_ref[...] + part)


def _matmul(a, w, out_dtype, *, bm=1024, bn=512, bk=None, a_cols=None, gain=None, epilogue=None):
    m = a.shape[0]
    k, n = w.shape
    bm = _pick(m, bm)
    bn = min(bn, n)
    bk = k if bk is None else _pick(k, bk)
    nk = k // bk
    col0 = 0
    if a_cols is not None:
        assert a_cols[1] == k and a_cols[0] % bk == 0
        col0 = a_cols[0] // bk
    prologue = None
    args = [a, w]
    specs = [pl.BlockSpec((bm, bk), lambda i, j, kk: (i, col0 + kk)),
             pl.BlockSpec((bk, bn), lambda i, j, kk: (kk, j))]
    if gain is not None:
        assert nk == 1
        prologue = "rms"
        args.append(gain.reshape(1, k))
        specs.append(pl.BlockSpec((1, k), lambda i, j, kk: (0, 0)))
    return pl.pallas_call(
        functools.partial(_mm_kernel, nk=nk, prologue=prologue, epilogue=epilogue),
        out_shape=jax.ShapeDtypeStruct((m, n), out_dtype),
        grid=(m // bm, pl.cdiv(n, bn), nk),
        in_specs=specs,
        out_specs=pl.BlockSpec((bm, bn), lambda i, j, kk: (i, j)),
        scratch_shapes=[pltpu.VMEM((bm, bn), F32)] if nk > 1 else [],
        compiler_params=_cp("arbitrary", "arbitrary", "arbitrary"),
        name="matmul",
    )(*args)


def _rope128(x, cosf, sins):
    lane = lax.broadcasted_iota(jnp.int32, x.shape, 1)
    first = (lane & 63) < 32
    swapped = jnp.where(first, pltpu.roll(x, 96, 1), pltpu.roll(x, 32, 1))
    return x * cosf + swapped * sins


def _qhead_kernel(qn_ref, qr_ref, wuk_ref, cos_ref, sin_ref, ql_ref, qp_ref, *, qscale):
    ql = _dot(qn_ref[...].astype(BF16), wuk_ref[0].astype(BF16))
    ql_ref[...] = (ql * qscale).astype(ql_ref.dtype)
    qp_ref[...] = (_rope128(qr_ref[...], cos_ref[...], sin_ref[...]) * qscale).astype(qp_ref.dtype)


def _qhead(q, wuk_t, cosf, sins, *, heads, qscale):
    m = q.shape[0]
    _, nope, rank = wuk_t.shape
    bm = _pick(m, 1024)
    return pl.pallas_call(
        functools.partial(_qhead_kernel, qscale=qscale),
        out_shape=[jax.ShapeDtypeStruct((m, heads * rank), BF16),
                   jax.ShapeDtypeStruct((m, heads * LANES), BF16)],
        grid=(m // bm, heads),
        in_specs=[pl.BlockSpec((bm, nope), lambda i, h: (i, h)),
                  pl.BlockSpec((bm, LANES), lambda i, h: (i, heads + h)),
                  pl.BlockSpec((1, nope, rank), lambda i, h: (h, 0, 0)),
                  pl.BlockSpec((bm, LANES), lambda i, h: (i, 0)),
                  pl.BlockSpec((bm, LANES), lambda i, h: (i, 0))],
        out_specs=[pl.BlockSpec((bm, rank), lambda i, h: (i, h)),
                   pl.BlockSpec((bm, LANES), lambda i, h: (i, h))],
        compiler_params=_cp("arbitrary", "arbitrary"),
        name="qhead",
    )(q, q, wuk_t, cosf, sins)


def _headmm_kernel(a_ref, w_ref, o_ref):
    o_ref[...] = _dot(a_ref[...].astype(BF16), w_ref[0].astype(BF16)).astype(o_ref.dtype)


def _headmm(a, w, out_dtype):
    m = a.shape[0]
    heads, k, n = w.shape
    bm = _pick(m, 1024)
    return pl.pallas_call(
        _headmm_kernel,
        out_shape=jax.ShapeDtypeStruct((m, heads * n), out_dtype),
        grid=(m // bm, heads),
        in_specs=[pl.BlockSpec((bm, k), lambda i, h: (i, h)),
                  pl.BlockSpec((1, k, n), lambda i, h: (h, 0, 0))],
        out_specs=pl.BlockSpec((bm, n), lambda i, h: (i, h)),
        compiler_params=_cp("arbitrary", "arbitrary"),
        name="headmm",
    )(a, w)


def _kvrows_kernel(z_ref, g_ref, cos_ref, sin_ref, of_ref, ob_ref, *, rank):
    z = z_ref[...]
    lat = _rms(z[:, :rank]) * g_ref[...]
    pe = _rope128(z[:, rank:], cos_ref[...], sin_ref[...])
    of_ref[:, :rank] = lat
    of_ref[:, rank:] = pe
    ob_ref[:, :rank] = lat.astype(BF16)
    ob_ref[:, rank:] = pe.astype(BF16)


def _kvrows(z, col_blk, g_kv, cosf, sins, *, rank):
    m = z.shape[0]
    w = rank + LANES
    bm = _pick(m, 1024)
    return pl.pallas_call(
        functools.partial(_kvrows_kernel, rank=rank),
        out_shape=[jax.ShapeDtypeStruct((m, w), F32), jax.ShapeDtypeStruct((m, w), BF16)],
        grid=(m // bm,),
        in_specs=[pl.BlockSpec((bm, w), lambda i: (i, col_blk)),
                  pl.BlockSpec((1, rank), lambda i: (0, 0)),
                  pl.BlockSpec((bm, LANES), lambda i: (i, 0)),
                  pl.BlockSpec((bm, LANES), lambda i: (i, 0))],
        out_specs=[pl.BlockSpec((bm, w), lambda i: (i, 0)), pl.BlockSpec((bm, w), lambda i: (i, 0))],
        compiler_params=_cp("arbitrary"),
        name="kvrows",
    )(z, g_kv.reshape(1, rank), cosf, sins)


def _flash_update(s, rc, v, m_ref, l_ref, acc_ref, rows=slice(None), v_transposed=False):
    m_old = m_ref[rows]
    m_new = jnp.maximum(m_old, jnp.max(s, axis=-1, keepdims=True) + rc)
    alpha = jnp.exp2(m_old - m_new)
    p = jnp.exp2(s - (m_new - rc))
    l_ref[rows] = alpha * l_ref[rows] + jnp.sum(p, axis=-1, keepdims=True)
    pb = p.astype(BF16)
    acc_ref[rows] = alpha * acc_ref[rows] + (_dot_nt(pb, v) if v_transposed else _dot(pb, v))
    m_ref[rows] = m_new


def _flash_init(m_ref, l_ref, acc_ref):
    m_ref[...] = jnp.full(m_ref.shape, NEG_INF, F32)
    l_ref[...] = jnp.zeros(l_ref.shape, F32)
    acc_ref[...] = jnp.zeros(acc_ref.shape, F32)


def _lambda_value(lam_ref):
    lv = lam_ref[...]
    d1 = jnp.sum(lv[0:1] * lv[1:2], axis=-1, keepdims=True)
    d2 = jnp.sum(lv[2:3] * lv[3:4], axis=-1, keepdims=True)
    return jnp.exp(d1) - jnp.exp(d2) + DIFF_LAMBDA_INIT


def _split_halves(q):
    lane = lax.broadcasted_iota(jnp.int32, q.shape, 1)
    lo = lane < (q.shape[1] // 2)
    zero = jnp.zeros_like(q)
    return jnp.where(lo, q, zero), jnp.where(lo, zero, q)


def _diff_finish(acc1, l1, acc2, l2, lam, g):
    a = acc1 / l1 - lam * (acc2 / l2)
    return (_rms(a) * g) * (1.0 - DIFF_LAMBDA_INIT)


def _flash_update_t(st, rc, vt, m_ref, l_ref, acc_ref, cols=slice(None)):
    m_old = m_ref[:, cols]
    m_new = jnp.maximum(m_old, jnp.max(st, axis=0, keepdims=True) + rc)
    alpha = jnp.exp2(m_old - m_new)
    p = jnp.exp2(st - (m_new - rc))
    l_ref[:, cols] = alpha * l_ref[:, cols] + jnp.sum(p, axis=0, keepdims=True)
    acc_ref[:, cols] = alpha * acc_ref[:, cols] + _dot(vt, p.astype(BF16))
    m_ref[:, cols] = m_new


def _causal_keep_t(tq, cols):
    key = lax.broadcasted_iota(jnp.int32, (tq, cols), 0)
    pos = lax.broadcasted_iota(jnp.int32, (tq, cols), 1) & (tq - 1)
    return key <= pos


def _rel_bias_t(tq):
    key = lax.broadcasted_iota(jnp.int32, (tq, tq), 0)
    pos = lax.broadcasted_iota(jnp.int32, (tq, tq), 1)
    return (key - pos).astype(F32)


def _transpose_bf16(x):
    return x.astype(F32).T.astype(BF16)


def _mla_prompt_kernel(ql_ref, qp_ref, k_ref, o_ref, qs, kt, m_ref, l_ref, acc_ref, *, tq, nq, heads, rank):
    qi = pl.program_id(1)

    @pl.when(qi == 0)
    def _():
        for c in range(nq):
            kt[c] = _transpose_bf16(k_ref[c * tq:(c + 1) * tq, :rank])

    for h in range(heads):
        rows = slice(h * tq, (h + 1) * tq)
        qs[rows, :rank] = ql_ref[:, h * rank:(h + 1) * rank]
        qs[rows, rank:] = qp_ref[:, h * LANES:(h + 1) * LANES]
    _flash_init(m_ref, l_ref, acc_ref)

    def body(c, carry):
        kc = k_ref[pl.ds(pl.multiple_of(c * tq, tq), tq), :]
        _flash_update_t(_dot_nt(kc, qs[...]), 0.0, kt[c], m_ref, l_ref, acc_ref)
        return carry

    lax.fori_loop(0, qi, body, 0)
    kc = k_ref[pl.ds(pl.multiple_of(qi * tq, tq), tq), :]
    st = jnp.where(_causal_keep_t(tq, heads * tq), _dot_nt(kc, qs[...]), NEG_INF)
    _flash_update_t(st, 0.0, kt[qi], m_ref, l_ref, acc_ref)
    for h in range(heads):
        cols = slice(h * tq, (h + 1) * tq)
        o_ref[:, h * rank:(h + 1) * rank] = (acc_ref[:, cols] / l_ref[:, cols]).T.astype(o_ref.dtype)


def _mla_prompt(q_lat, q_pe, rows_bf, *, b, t, heads, rank):
    tq = Q_TILE
    assert t % tq == 0
    nq = t // tq
    w = rank + LANES
    r = heads * tq
    return pl.pallas_call(
        functools.partial(_mla_prompt_kernel, tq=tq, nq=nq, heads=heads, rank=rank),
        out_shape=jax.ShapeDtypeStruct((b * t, heads * rank), BF16),
        grid=(b, nq),
        in_specs=[pl.BlockSpec((tq, heads * rank), lambda bb, i: (bb * nq + i, 0)),
                  pl.BlockSpec((tq, heads * LANES), lambda bb, i: (bb * nq + i, 0)),
                  pl.BlockSpec((t, w), lambda bb, i: (bb, 0))],
        out_specs=pl.BlockSpec((tq, heads * rank), lambda bb, i: (bb * nq + i, 0)),
        scratch_shapes=[pltpu.VMEM((r, w), BF16), pltpu.VMEM((nq, rank, tq), BF16),
                        pltpu.VMEM((1, r), F32), pltpu.VMEM((1, r), F32), pltpu.VMEM((rank, r), F32)],
        compiler_params=_cp("arbitrary", "arbitrary"),
        name="mla_prompt",
    )(q_lat, q_pe, rows_bf)


def _diff_prompt_kernel(slopes_ref, q_ref, k_ref, v_ref, lam_ref, g_ref, o_ref,
                        qs, kb, vt, relb, slr, m_ref, l_ref, acc_ref, *, tq, nq, heads, dv, qscale):
    qi = pl.program_id(1)
    hr = heads * tq

    @pl.when(qi == 0)
    def _():
        kb[...] = k_ref[...].astype(BF16)
        for c in range(nq):
            vt[c] = v_ref[c * tq:(c + 1) * tq, :].T.astype(BF16)
        rel = _rel_bias_t(tq)
        for h in range(heads):
            s2 = slopes_ref[h] * LOG2E
            relb[:, h * tq:(h + 1) * tq] = s2 * rel
            slr[:, h * tq:(h + 1) * tq] = jnp.full((1, tq), s2, F32)

    for h in range(heads):
        q1, q2 = _split_halves((q_ref[:, h * LANES:(h + 1) * LANES] * qscale).astype(BF16))
        qs[h * tq:(h + 1) * tq, :] = q1
        qs[hr + h * tq:hr + (h + 1) * tq, :] = q2
    _flash_init(m_ref, l_ref, acc_ref)
    halves = (slice(0, hr), slice(hr, 2 * hr))

    def body(c, carry):
        k = kb[pl.ds(pl.multiple_of(c * tq, tq), tq), :]
        rc = slr[...] * ((c - qi) * tq).astype(F32)
        for half in halves:
            _flash_update_t(_dot_nt(k, qs[half]) + relb[...], rc, vt[c], m_ref, l_ref, acc_ref, half)
        return carry

    lax.fori_loop(0, qi, body, 0)
    k = kb[pl.ds(pl.multiple_of(qi * tq, tq), tq), :]
    keep = _causal_keep_t(tq, hr)
    for half in halves:
        st = jnp.where(keep, _dot_nt(k, qs[half]) + relb[...], NEG_INF)
        _flash_update_t(st, 0.0, vt[qi], m_ref, l_ref, acc_ref, half)

    lam = _lambda_value(lam_ref)
    for h in range(heads):
        c1 = slice(h * tq, (h + 1) * tq)
        c2 = slice(hr + h * tq, hr + (h + 1) * tq)
        a = acc_ref[:, c1] / l_ref[:, c1] - lam * (acc_ref[:, c2] / l_ref[:, c2])
        o_ref[:, h * dv:(h + 1) * dv] = (
            (_rms(a.T) * g_ref[...]) * (1.0 - DIFF_LAMBDA_INIT)).astype(o_ref.dtype)


def _diff_prompt(slopes, z0, lam_in, g_sub, *, b, t, heads, k_blk, v_blk, dv, qscale):
    tq = Q_TILE
    assert t % tq == 0
    nq = t // tq
    hr = heads * tq
    return pl.pallas_call(
        functools.partial(_diff_prompt_kernel, tq=tq, nq=nq, heads=heads, dv=dv, qscale=qscale),
        out_shape=jax.ShapeDtypeStruct((b * t, heads * dv), BF16),
        grid=(b, nq),
        in_specs=[pl.BlockSpec(memory_space=pltpu.SMEM),
                  pl.BlockSpec((tq, heads * LANES), lambda bb, i: (bb * nq + i, 0)),
                  pl.BlockSpec((t, LANES), lambda bb, i: (bb, k_blk)),
                  pl.BlockSpec((t, dv), lambda bb, i: (bb, v_blk)),
                  pl.BlockSpec((8, LANES), lambda bb, i: (0, 0)),
                  pl.BlockSpec((1, dv), lambda bb, i: (0, 0))],
        out_specs=pl.BlockSpec((tq, heads * dv), lambda bb, i: (bb * nq + i, 0)),
        scratch_shapes=[pltpu.VMEM((2 * hr, LANES), BF16), pltpu.VMEM((t, LANES), BF16),
                        pltpu.VMEM((nq, dv, tq), BF16), pltpu.VMEM((tq, hr), F32), pltpu.VMEM((1, hr), F32),
                        pltpu.VMEM((1, 2 * hr), F32), pltpu.VMEM((1, 2 * hr), F32),
                        pltpu.VMEM((dv, 2 * hr), F32)],
        compiler_params=_cp("arbitrary", "arbitrary"),
        name="diff_prompt",
    )(slopes, z0, z0, z0, lam_in, g_sub.reshape(1, dv))


def _ab_sample_kernel(pt_ref, ql_ref, qp_ref, qd_ref, kn_ref, dkn_ref, dvn_ref, slope_ref, t_ref,
                      tab_ref, lam_ref, g_ref, *rest, npg, page, rank, rope, past, nchunk, qscale_d):
    mla_pg = rest[:npg]
    dk_pg = rest[npg:2 * npg]
    dv_pg = rest[2 * npg:3 * npg]
    ol_ref, od_ref = rest[3 * npg:3 * npg + 2]
    kt, dkb, dvb, m_a, l_a, acc_a, m_d, l_d, acc_d = rest[3 * npg + 2:]
    j = pl.program_id(1)
    kc = npg * page
    row_w = rank + rope

    @pl.when(j == 0)
    def _():
        _flash_init(m_a, l_a, acc_a)
        _flash_init(m_d, l_d, acc_d)
        kt[row_w:, :] = jnp.zeros((kt.shape[0] - row_w, kc), BF16)

    for p in range(npg):
        cols = slice(p * page, (p + 1) * page)
        kt[:row_w, cols] = mla_pg[p][0].astype(BF16)
        dkb[cols, :] = dk_pg[p][0].astype(BF16)
        dvb[cols, :] = dv_pg[p][0].astype(BF16)

    ql = ql_ref[0]
    qp = qp_ref[0]
    q1, q2 = _split_halves((qd_ref[0] * qscale_d).astype(BF16))
    qd = jnp.concatenate([q1, q2], axis=0)
    slope2 = slope_ref[...] * LOG2E
    tq = t_ref[...]
    r = ql.shape[0]

    lat_t = kt[:rank, :]
    _flash_update(_dot(ql, lat_t) + _dot(qp, kt[rank:, :]), 0.0, lat_t, m_a, l_a, acc_a, v_transposed=True)

    rc = slope2 * ((j * kc - past).astype(F32) - tq)
    _flash_update(_dot_nt(qd, dkb[...]) + tab_ref[...], rc, dvb[...], m_d, l_d, acc_d)

    @pl.when(j == nchunk - 1)
    def _():
        kn = kn_ref[0]
        nk = kn.shape[0]
        tk = lax.broadcasted_iota(jnp.int32, (2 * r, nk), 1).astype(F32)
        keep = tk <= tq
        latn = kn[:, :rank]
        sn = _dot_nt(ql, latn) + _dot_nt(qp, kn[:, rank:])
        _flash_update(jnp.where(keep[:r], sn, NEG_INF), 0.0, latn, m_a, l_a, acc_a)
        sd = _dot_nt(qd, dkn_ref[0]) + slope2 * (tk - tq)
        _flash_update(jnp.where(keep, sd, NEG_INF), 0.0, dvn_ref[0], m_d, l_d, acc_d)
        ol_ref[0] = (acc_a[...] / l_a[...]).astype(ol_ref.dtype)
        od_ref[0] = _diff_finish(acc_d[:r], l_d[:r], acc_d[r:], l_d[r:],
                                 _lambda_value(lam_ref), g_ref[...]).astype(od_ref.dtype)


def _pages_per_step(n_pages, most):
    for p in (32, 16, 8, 4, 2, 1):
        if p > most:
            continue
        if n_pages % p == 0 and n_pages // p >= 2:
            return p
    return 1


def _ab_sample(pt_flat, ql, qp, qd, kn, dkn, dvn, slope_col, t_col, lam_in, g_sub,
               cache_mla, cache_dk, cache_dv, *, n_pages, rank, rope, qscale_d):
    db, r, _ = ql.shape
    row_w, page = cache_mla.shape[1:]
    dv = cache_dv.shape[2]
    npg = _pages_per_step(n_pages, AB_PAGES_PER_STEP)
    nchunk = n_pages // npg
    kc = npg * page
    past = n_pages * page
    slope2 = jnp.concatenate([slope_col, slope_col], axis=0)
    t2 = jnp.concatenate([t_col, t_col], axis=0)
    tab = (slope2 * LOG2E) * jnp.arange(kc, dtype=F32)[None, :]

    def page_spec(shape, p):
        return pl.BlockSpec((1,) + shape, lambda b, j, pt: (pt[b * n_pages + j * npg + p], 0, 0))

    def seq_spec(shape):
        return pl.BlockSpec((1,) + shape, lambda b, j, pt: (b, 0, 0))

    def const_spec(shape):
        return pl.BlockSpec(shape, lambda b, j, pt: (0, 0))

    in_specs = [seq_spec((r, rank)), seq_spec((r, LANES)), seq_spec((r, LANES)),
                seq_spec((NEW_KEY_PAD, rank + LANES)), seq_spec((NEW_KEY_PAD, LANES)),
                seq_spec((NEW_KEY_PAD, dv)),
                const_spec((2 * r, 1)), const_spec((2 * r, 1)),
                pl.BlockSpec((2 * r, kc), lambda b, j, pt: (0, 0), pipeline_mode=pl.Buffered(1)),
                const_spec((8, LANES)), const_spec((1, dv))]
    in_specs += [page_spec((row_w, page), p) for p in range(npg)]
    in_specs += [page_spec((page, LANES), p) for p in range(npg)]
    in_specs += [page_spec((page, dv), p) for p in range(npg)]
    scratch = [pltpu.VMEM((rank + LANES, kc), BF16),
               pltpu.VMEM((kc, LANES), BF16), pltpu.VMEM((kc, dv), BF16),
               pltpu.VMEM((r, 1), F32), pltpu.VMEM((r, 1), F32), pltpu.VMEM((r, rank), F32),
               pltpu.VMEM((2 * r, 1), F32), pltpu.VMEM((2 * r, 1), F32), pltpu.VMEM((2 * r, dv), F32)]
    return pl.pallas_call(
        functools.partial(_ab_sample_kernel, npg=npg, page=page, rank=rank, rope=rope, past=past,
                          nchunk=nchunk, qscale_d=qscale_d),
        out_shape=[jax.ShapeDtypeStruct((db, r, rank), BF16), jax.ShapeDtypeStruct((db, r, dv), BF16)],
        grid_spec=pltpu.PrefetchScalarGridSpec(
            num_scalar_prefetch=1, grid=(db, nchunk), in_specs=in_specs,
            out_specs=[seq_spec((r, rank)), seq_spec((r, dv))], scratch_shapes=scratch),
        compiler_params=_cp("arbitrary", "arbitrary"),
        name="ab_sample",
    )(pt_flat, ql, qp, qd, kn, dkn, dvn, slope2, t2, tab, lam_in, g_sub.reshape(1, dv),
      *([cache_mla] * npg), *([cache_dk] * npg), *([cache_dv] * npg))


def _topk_mask_t(gate_ref, nblk, n_valid, n_sel, always):
    gate = gate_ref[...]
    blk = lax.broadcasted_iota(jnp.int32, gate.shape, 0)

    def body(k, rank):
        gk = gate_ref[pl.ds(k, 1), :]
        beats = jnp.logical_or(gk > gate, jnp.logical_and(gk == gate, k < blk))
        return rank + jnp.where(jnp.logical_and(beats, k < n_valid), 1.0, 0.0)

    rank = lax.fori_loop(0, nblk, body, jnp.zeros(gate.shape, F32))
    chosen = jnp.logical_and(blk < n_valid, rank < float(n_sel))
    return jnp.where(jnp.logical_or(chosen, blk == always), 0.0, MASKED)


def _moba_prompt_kernel(slopes_ref, q_ref, k_ref, v_ref, o_ref,
                        qs, kaug, vt, kmean, relb, slr, gsc, m_ref, l_ref, acc_ref,
                        *, blk, nblk, rh, d, qscale):
    g = pl.program_id(1)
    qi = pl.program_id(2)
    hr = rh * blk

    @pl.when(qi == 0)
    def _():
        t = k_ref.shape[0]
        kaug[:, :d] = k_ref[...].astype(BF16)
        krow_blk = lax.broadcasted_iota(jnp.int32, (t, d), 0) >> BLOCK_SHIFT
        lane = lax.broadcasted_iota(jnp.int32, (t, d), 1)
        kaug[:, d:] = jnp.where((lane & 7) == krow_blk, 1.0, 0.0).astype(BF16)
        kmean[...] = jnp.zeros(kmean.shape, F32)
        for n in range(nblk):
            vt[n] = v_ref[n * blk:(n + 1) * blk, :].T.astype(BF16)
            kmean[n:n + 1, :] = jnp.mean(k_ref[n * blk:(n + 1) * blk, :], axis=0, keepdims=True)
        rel = _rel_bias_t(blk)
        for h in range(rh):
            s2 = slopes_ref[g * rh + h] * LOG2E
            relb[:, h * blk:(h + 1) * blk] = s2 * rel
            slr[:, h * blk:(h + 1) * blk] = jnp.full((1, blk), s2, F32)

    km = kmean[...]
    for h in range(rh):
        qh = q_ref[:, h * d:(h + 1) * d]
        qs[h * blk:(h + 1) * blk, :d] = (qh * qscale).astype(BF16)
        gsc[h * SUBLANES:(h + 1) * SUBLANES, :] = _dot3_nt(km, qh)

    gate = gsc[...].reshape(rh, SUBLANES, blk)
    bidx = lax.broadcasted_iota(jnp.int32, gate.shape, 1)
    rank = jnp.zeros(gate.shape, F32)
    for k in range(nblk):
        gk = gate[:, k:k + 1, :]
        beats = jnp.logical_or(gk > gate, jnp.logical_and(gk == gate, k < bidx))
        rank = rank + jnp.where(jnp.logical_and(beats, k < qi), 1.0, 0.0)
    chosen = jnp.logical_and(bidx < qi, rank < float(min(MOBA_TOPK, nblk - 1)))
    sel = jnp.where(jnp.logical_or(chosen, bidx == qi), 0.0, MASKED)
    sel_rows = sel.reshape(rh * SUBLANES, blk).T.astype(BF16)
    lane_head = lax.broadcasted_iota(jnp.int32, sel_rows.shape, 1) >> 3
    zero = jnp.zeros_like(sel_rows)
    for h in range(rh):
        qs[h * blk:(h + 1) * blk, d:] = jnp.where(lane_head == h, sel_rows, zero)

    _flash_init(m_ref, l_ref, acc_ref)
    own = pl.multiple_of(qi * blk, blk)
    st = jnp.where(_causal_keep_t(blk, hr), _dot_nt(kaug[pl.ds(own, blk), :], qs[...]) + relb[...], NEG_INF)
    _flash_update_t(st, 0.0, vt[qi], m_ref, l_ref, acc_ref)

    def body(jb, carry):
        off = pl.multiple_of(jb * blk, blk)
        rc = slr[...] * ((jb - qi) * blk).astype(F32)
        _flash_update_t(_dot_nt(kaug[pl.ds(off, blk), :], qs[...]) + relb[...], rc, vt[jb],
                        m_ref, l_ref, acc_ref)
        return carry

    lax.fori_loop(0, qi, body, 0)
    for h in range(rh):
        cols = slice(h * blk, (h + 1) * blk)
        o_ref[:, h * d:(h + 1) * d] = (acc_ref[:, cols] / l_ref[:, cols]).T.astype(o_ref.dtype)


def _moba_prompt(slopes, zc, *, b, t, heads, groups, d, k_blk, v_blk, qscale):
    blk = MOBA_BLOCK
    assert t % blk == 0
    nblk = t // blk
    rh = heads // groups
    assert nblk <= SUBLANES and rh * SUBLANES == LANES and d == LANES
    hr = rh * blk
    return pl.pallas_call(
        functools.partial(_moba_prompt_kernel, blk=blk, nblk=nblk, rh=rh, d=d, qscale=qscale),
        out_shape=jax.ShapeDtypeStruct((b * t, heads * d), BF16),
        grid=(b, groups, nblk),
        in_specs=[pl.BlockSpec(memory_space=pltpu.SMEM),
                  pl.BlockSpec((blk, rh * d), lambda bb, g, i: (bb * nblk + i, g)),
                  pl.BlockSpec((t, d), lambda bb, g, i: (bb, k_blk + g)),
                  pl.BlockSpec((t, d), lambda bb, g, i: (bb, v_blk + g))],
        out_specs=pl.BlockSpec((blk, rh * d), lambda bb, g, i: (bb * nblk + i, g)),
        scratch_shapes=[pltpu.VMEM((hr, 2 * d), BF16), pltpu.VMEM((t, 2 * d), BF16),
                        pltpu.VMEM((nblk, d, blk), BF16),
                        pltpu.VMEM((SUBLANES, d), F32), pltpu.VMEM((blk, hr), F32), pltpu.VMEM((1, hr), F32),
                        pltpu.VMEM((rh * SUBLANES, blk), F32),
                        pltpu.VMEM((1, hr), F32), pltpu.VMEM((1, hr), F32), pltpu.VMEM((d, hr), F32)],
        compiler_params=_cp("arbitrary", "arbitrary", "arbitrary"),
        name="moba_prompt",
    )(slopes, zc, zc, zc)


def _moba_sample_kernel(pt_ref, q_ref, kn_ref, vn_ref, slope_ref, t_ref, tab_ref, *rest,
                        npg, page, groups, d, past, nchunk, nfull, qscale):
    k_pg = rest[:npg]
    v_pg = rest[npg:2 * npg]
    o_ref = rest[2 * npg]
    s_all, kb, vb, ksum, gsc, selb, m_s, l_s, acc_s = rest[2 * npg + 1:]
    j = pl.program_id(1)
    kc = npg * page
    ppb = MOBA_BLOCK // page
    bpc = kc // MOBA_BLOCK
    r = q_ref.shape[2]
    tq = t_ref[...]

    @pl.when(j == 0)
    def _():
        ksum[...] = jnp.zeros(ksum.shape, F32)

    @pl.when(j < nchunk)
    def _():
        for g in range(groups):
            csum = None
            for p in range(npg):
                x = k_pg[p][0, pl.ds(g, page, stride=groups), :]
                kb[p * page:(p + 1) * page, g * d:(g + 1) * d] = x.astype(BF16)
                cs = jnp.sum(x, axis=0, keepdims=True)
                csum = cs if p % ppb == 0 else csum + cs
                if p % ppb == ppb - 1:
                    ksum[g, pl.ds(j * bpc + p // ppb, 1), :] = csum
        for g in range(groups):
            qb = (q_ref[0, g] * qscale).astype(BF16)
            s_all[g, j] = _dot_nt(qb, kb[:, g * d:(g + 1) * d]) + tab_ref[g]

    @pl.when(j == nchunk - 1)
    def _():
        kn = kn_ref[0]
        vn = vn_ref[0]
        nk = kn.shape[0]
        tk = lax.broadcasted_iota(jnp.int32, (r, nk), 1).astype(F32)
        keep = tk <= tq
        for g in range(groups):
            q = q_ref[0, g]
            gsc[...] = _dot3_nt(ksum[g] * (1.0 / MOBA_BLOCK), q)
            selb[g] = _topk_mask_t(gsc, nfull, nfull, min(MOBA_TOPK, nfull), -1).T.astype(BF16)
            m_ref, l_ref, acc_ref = m_s.at[g], l_s.at[g], acc_s.at[g]
            _flash_init(m_ref, l_ref, acc_ref)
            sn = _dot_nt((q * qscale).astype(BF16), kn[:, g * d:(g + 1) * d]) + (slope_ref[g] * LOG2E) * (tk - tq)
            _flash_update(jnp.where(keep, sn, NEG_INF), 0.0, vn[:, g * d:(g + 1) * d], m_ref, l_ref, acc_ref)

    @pl.when(j >= nchunk)
    def _():
        c = j - nchunk
        for g in range(groups):
            for p in range(npg):
                vb[p * page:(p + 1) * page, g * d:(g + 1) * d] = (
                    v_pg[p][0, pl.ds(g, page, stride=groups), :].astype(BF16))
        kblk = c * bpc + (lax.broadcasted_iota(jnp.int32, (LANES, kc), 1) >> BLOCK_SHIFT)
        ind = jnp.where(lax.broadcasted_iota(jnp.int32, (LANES, kc), 0) == kblk, 1.0, 0.0).astype(BF16)
        for g in range(groups):
            rc = (slope_ref[g] * LOG2E) * ((c * kc - past).astype(F32) - tq)
            _flash_update(s_all[g, c] + _dot(selb[g], ind), rc, vb[:, g * d:(g + 1) * d],
                          m_s.at[g], l_s.at[g], acc_s.at[g])

    @pl.when(j == 2 * nchunk - 1)
    def _():
        for g in range(groups):
            o_ref[0, g] = (acc_s[g] / l_s[g]).astype(o_ref.dtype)


def _moba_sample(pt_flat, q, kn, vn, slope_col, t_col, cache_k, cache_v, *, n_pages, groups, d, qscale):
    db, _, r, _ = q.shape
    page = cache_k.shape[1] // groups
    assert MOBA_BLOCK % page == 0
    past = n_pages * page
    assert past % MOBA_BLOCK == 0
    nfull = past // MOBA_BLOCK
    assert 1 <= nfull <= LANES
    npg = _pages_per_step(n_pages, MOBA_PAGES_PER_STEP)
    while (npg * page) % MOBA_BLOCK:
        npg *= 2
    nchunk = n_pages // npg
    kc = npg * page
    gd = groups * d
    tab = (slope_col * LOG2E) * jnp.arange(kc, dtype=F32)[None, None, :]

    def kspec(p):
        return pl.BlockSpec((1, page * groups, d), lambda b, j, pt:
                            (pt[b * n_pages + jnp.minimum(j, nchunk - 1) * npg + p], 0, 0))

    def vspec(p):
        return pl.BlockSpec((1, page * groups, d), lambda b, j, pt:
                            (pt[b * n_pages + jnp.maximum(j - nchunk, 0) * npg + p], 0, 0))

    in_specs = [pl.BlockSpec((1, groups, r, d), lambda b, j, pt: (b, 0, 0, 0)),
                pl.BlockSpec((1, NEW_KEY_PAD, gd), lambda b, j, pt: (b, 0, 0)),
                pl.BlockSpec((1, NEW_KEY_PAD, gd), lambda b, j, pt: (b, 0, 0)),
                pl.BlockSpec((groups, r, 1), lambda b, j, pt: (0, 0, 0)),
                pl.BlockSpec((r, 1), lambda b, j, pt: (0, 0)),
                pl.BlockSpec((groups, r, kc), lambda b, j, pt: (0, 0, 0), pipeline_mode=pl.Buffered(1))]
    in_specs += [kspec(p) for p in range(npg)] + [vspec(p) for p in range(npg)]
    scratch = [pltpu.VMEM((groups, nchunk, r, kc), F32),
               pltpu.VMEM((kc, gd), BF16), pltpu.VMEM((kc, gd), BF16),
               pltpu.VMEM((groups, LANES, d), F32), pltpu.VMEM((LANES, r), F32),
               pltpu.VMEM((groups, r, LANES), BF16),
               pltpu.VMEM((groups, r, 1), F32), pltpu.VMEM((groups, r, 1), F32),
               pltpu.VMEM((groups, r, d), F32)]
    return pl.pallas_call(
        functools.partial(_moba_sample_kernel, npg=npg, page=page, groups=groups, d=d, past=past,
                          nchunk=nchunk, nfull=nfull, qscale=qscale),
        out_shape=jax.ShapeDtypeStruct((db, groups, r, d), BF16),
        grid_spec=pltpu.PrefetchScalarGridSpec(
            num_scalar_prefetch=1, grid=(db, 2 * nchunk), in_specs=in_specs,
            out_specs=pl.BlockSpec((1, groups, r, d), lambda b, j, pt: (b, 0, 0, 0)),
            scratch_shapes=scratch),
        compiler_params=_cp("arbitrary", "arbitrary"),
        name="moba_sample",
    )(pt_flat, q, kn, vn, slope_col, t_col, tab, *([cache_k] * npg), *([cache_v] * npg))


def _alibi_slopes(n):
    return np.asarray(2.0 ** (-8.0 * np.arange(1, n + 1) / n), dtype=np.float32)


def _rope_tables(pos, half):
    inv = ROPE_THETA ** (-jnp.arange(half, dtype=F32) / half)
    ang = pos.astype(F32)[:, None] * inv
    cos, sin = jnp.cos(ang), jnp.sin(ang)
    reps = LANES // (2 * half)
    return jnp.tile(cos, (1, 2 * reps)), jnp.tile(jnp.concatenate([-sin, sin], axis=-1), (1, reps))


def _pad_new(x, db, ts):
    x = x.reshape(db, ts, x.shape[-1]).astype(BF16)
    return jnp.pad(x, ((0, 0), (0, NEW_KEY_PAD - ts), (0, 0)))


def kernel(x_prompt, x_sample, cache_mla, cache_diff_k, cache_diff_v, cache_moba_k, cache_moba_v,
           page_table, c_prompt, c_sample, w_ada, b_ada, g_pre_mix, g_post_mix, g_pre_ffn, g_post_ffn,
           w_in_ab, g_q_a, w_q_b, g_kv_a, w_uk, w_uv, lambda_q1, lambda_k1, lambda_q2, lambda_k2,
           g_diff_sub, w_out_ab, w_in_c, w_out_c, w_ff_up, w_ff_down):
    b, t, d_model = x_prompt.shape
    db, ts, _ = x_sample.shape
    mp, ms = b * t, db * ts
    m = mp + ms
    dims = (b, t, db, ts)
    n_pool, page, row_w = cache_mla.shape
    n_pages = page_table.shape[1]
    past = n_pages * page
    q_rank = g_q_a.shape[0]
    kv_rank, heads_a, nope = w_uk.shape
    rope = row_w - kv_rank
    dh = lambda_q1.shape[0]
    dv = g_diff_sub.shape[0]
    heads_b = (w_in_ab.shape[1] - q_rank - row_w - 2 * dh - dv) // (2 * dh)
    groups, d_c = cache_moba_k.shape[2], cache_moba_k.shape[3]
    heads_c = w_out_c.shape[0] // d_c
    rh = heads_c // groups
    assert nope == LANES and 2 * rope == LANES and 2 * dh == LANES and dv == LANES and d_c == LANES
    assert kv_rank % LANES == 0 and heads_a == heads_b and cache_diff_k.shape[2] == 1
    assert ts <= NEW_KEY_PAD

    o1, o2 = q_rank, q_rank + row_w
    o3 = o2 + heads_b * 2 * dh
    w_kv = jnp.pad(w_in_ab[:, o1:o2], ((0, 0), (0, LANES - rope)))
    w0 = jnp.concatenate([w_in_ab[:, o2:o3], w_in_ab[:, :o1], w_kv, w_in_ab[:, o3:]], axis=1)
    c_qa = heads_b * 2 * dh
    c_kv = c_qa + q_rank
    c_dk = c_kv + kv_rank + LANES
    c_dv = c_dk + 2 * dh
    assert c_kv % (kv_rank + LANES) == 0 and c_qa % q_rank == 0
    wq = w_q_b.reshape(q_rank, heads_a, nope + rope)
    wq_rope = jnp.pad(wq[:, :, nope:], ((0, 0), (0, 0), (0, LANES - rope)))
    w_q = jnp.concatenate([wq[:, :, :nope].reshape(q_rank, heads_a * nope),
                           wq_rope.reshape(q_rank, heads_a * LANES)], axis=1)
    wuk_t = jnp.transpose(w_uk, (1, 2, 0))
    wuv_t = jnp.transpose(w_uv, (1, 0, 2))
    pos_all = jnp.concatenate([jnp.tile(jnp.arange(t, dtype=jnp.int32), b),
                               jnp.tile(past + jnp.arange(ts, dtype=jnp.int32), db)])
    cosf, sins = _rope_tables(pos_all, rope // 2)
    lam_in = jnp.zeros((8, LANES), F32)
    for i, v in enumerate((lambda_q1, lambda_k1, lambda_q2, lambda_k2)):
        lam_in = lam_in.at[i, :dh].set(v.astype(F32))
    slopes_b = _alibi_slopes(heads_b)
    slopes_c = _alibi_slopes(heads_c)
    t_col = jnp.asarray(np.repeat(np.arange(ts, dtype=np.float32), heads_a).reshape(ts * heads_a, 1))
    slope_col_b = jnp.asarray(np.tile(slopes_b, ts).reshape(ts * heads_b, 1))
    t_col_c = jnp.asarray(np.repeat(np.arange(ts, dtype=np.float32), rh).reshape(ts * rh, 1))
    slope_col_c = jnp.asarray(np.tile(slopes_c.reshape(groups, 1, rh), (1, ts, 1)).reshape(groups, ts * rh, 1))
    pt_flat = page_table.reshape(-1)
    cache_dk = cache_diff_k.reshape(n_pool, page, 2 * dh)
    cache_dv = cache_diff_v.reshape(n_pool, page, dv)
    cache_mk = cache_moba_k.reshape(n_pool, page * groups, d_c)
    cache_mv = cache_moba_v.reshape(n_pool, page * groups, d_c)
    cache_mla_t = jnp.swapaxes(cache_mla, 1, 2)

    n_c = b + db
    c_all = jnp.pad(jnp.concatenate([c_prompt, c_sample], axis=0), ((0, -n_c % 16), (0, 0)))
    mod = _ada(c_all, w_ada, b_ada).reshape(w_ada.shape[0], c_all.shape[0], 6, d_model)
    mods = [(mod[l, :b], mod[l, b:n_c]) for l in range(mod.shape[0])]

    x = jnp.concatenate([x_prompt.reshape(mp, d_model), x_sample.reshape(ms, d_model)], axis=0)
    x = x.reshape(m // ts, ts, d_model)
    pp = functools.partial(_postpre, dims=dims)

    def ffn_mats(h, l):
        a = _matmul(h.reshape(m, d_model), w_ff_up[l], BF16, bm=1024, bn=512, epilogue="relu2")
        return _matmul(a, w_ff_down[l], F32, bm=1024, bn=1024, bk=2048).reshape(x.shape)

    (h,) = pp(x, None, None, None, mods[0], g_pre_mix[0], gate_idx=2, pre_idx=0)
    z0 = _matmul(h.reshape(m, d_model), w0, F32)
    q = _matmul(z0, w_q, F32, a_cols=(c_qa, q_rank), gain=g_q_a)
    qscale_a = (nope + rope) ** -0.5 * LOG2E
    qscale_d = dh ** -0.5 * LOG2E
    q_lat, q_pe = _qhead(q, wuk_t, cosf, sins, heads=heads_a, qscale=qscale_a)
    rows_f, rows_bf = _kvrows(z0, c_kv // (kv_rank + LANES), g_kv_a, cosf, sins, rank=kv_rank)

    ol_p = _mla_prompt(q_lat, q_pe, rows_bf, b=b, t=t, heads=heads_a, rank=kv_rank)
    od_p = _diff_prompt(jnp.asarray(slopes_b), z0, lam_in, g_diff_sub, b=b, t=t, heads=heads_b,
                        k_blk=c_dk // LANES, v_blk=c_dv // LANES, dv=dv, qscale=qscale_d)
    r_ab = ts * heads_a
    ol_s, od_s = _ab_sample(
        pt_flat,
        q_lat[mp:].reshape(db, r_ab, kv_rank), q_pe[mp:].reshape(db, r_ab, LANES),
        z0[mp:, :c_qa].reshape(db, r_ab, 2 * dh),
        _pad_new(rows_bf[mp:], db, ts), _pad_new(z0[mp:, c_dk:c_dv], db, ts), _pad_new(z0[mp:, c_dv:], db, ts),
        slope_col_b, t_col, lam_in, g_diff_sub, cache_mla_t, cache_dk, cache_dv,
        n_pages=n_pages, rank=kv_rank, rope=rope, qscale_d=qscale_d)
    o_lat = jnp.concatenate([ol_p, ol_s.reshape(ms, heads_a * kv_rank)], axis=0)
    o_d = jnp.concatenate([od_p, od_s.reshape(ms, heads_b * dv)], axis=0)
    o_cat = jnp.concatenate([_headmm(o_lat, wuv_t, BF16), o_d], axis=1)
    mix = _matmul(o_cat, w_out_ab, F32, bm=1024, bn=512)
    x, h = pp(x, mix.reshape(x.shape), mods[0], g_post_mix[0], mods[0], g_pre_ffn[0], gate_idx=2, pre_idx=3)
    x, h = pp(x, ffn_mats(h, 0), mods[0], g_post_ffn[0], mods[1], g_pre_mix[1], gate_idx=5, pre_idx=0)

    zc = _matmul(h.reshape(m, d_model), w_in_c, F32)
    nq_c = heads_c * d_c
    qscale_c = d_c ** -0.5 * LOG2E
    oc_p = _moba_prompt(jnp.asarray(slopes_c), zc, b=b, t=t, heads=heads_c, groups=groups, d=d_c,
                        k_blk=heads_c, v_blk=heads_c + groups, qscale=qscale_c)
    q_s = zc[mp:, :nq_c].reshape(db, ts, groups, rh, d_c).transpose(0, 2, 1, 3, 4).reshape(db, groups, ts * rh, d_c)
    oc_s = _moba_sample(pt_flat, q_s, _pad_new(zc[mp:, nq_c:nq_c + groups * d_c], db, ts),
                        _pad_new(zc[mp:, nq_c + groups * d_c:], db, ts), slope_col_c, t_col_c,
                        cache_mk, cache_mv, n_pages=n_pages, groups=groups, d=d_c, qscale=qscale_c)
    oc_s = oc_s.reshape(db, groups, ts, rh, d_c).transpose(0, 2, 1, 3, 4).reshape(ms, nq_c)
    mix = _matmul(jnp.concatenate([oc_p, oc_s], axis=0), w_out_c, F32, bm=1024, bn=512)
    x, h = pp(x, mix.reshape(x.shape), mods[1], g_post_mix[1], mods[1], g_pre_ffn[1], gate_idx=2, pre_idx=3)
    y = ffn_mats(h, 1)
    (y_p,) = pp(x, y, mods[1], g_post_ffn[1], None, None, gate_idx=5, pre_idx=0, part="prompt")
    (y_s,) = pp(x, y, mods[1], g_post_ffn[1], None, None, gate_idx=5, pre_idx=0, part="sample")

    kd = groups * d_c
    return (y_p.reshape(b, t, d_model), y_s.reshape(db, ts, d_model),
            rows_f[:mp, :row_w].reshape(b, t, row_w),
            z0[:mp, c_dk:c_dv].reshape(b, t, 1, 2 * dh), z0[:mp, c_dv:].reshape(b, t, 1, dv),
            zc[:mp, nq_c:nq_c + kd].reshape(b, t, groups, d_c), zc[:mp, nq_c + kd:].reshape(b, t, groups, d_c),
            rows_f[mp:, :row_w].reshape(db, ts, row_w),
            z0[mp:, c_dk:c_dv].reshape(db, ts, 1, 2 * dh), z0[mp:, c_dv:].reshape(db, ts, 1, dv),
            zc[mp:, nq_c:nq_c + kd].reshape(db, ts, groups, d_c), zc[mp:, nq_c + kd:].reshape(db, ts, groups, d_c))
```

```python
import functools
import math

import numpy as np
import jax
import jax.numpy as jnp
from jax import lax
from jax.experimental import pallas as pl
from jax.experimental.pallas import tpu as pltpu

F32 = jnp.float32
BF16 = jnp.bfloat16
EPS = 1e-6
ROPE_THETA = 10000.0
MOBA_BLOCK = 256
BLOCK_SHIFT = MOBA_BLOCK.bit_length() - 1
MOBA_TOPK = 3
DIFF_LAMBDA_INIT = 0.8 - 0.6 * math.exp(-0.3 * 0)
LOG2E = math.log2(math.e)
NEG_INF = float("-inf")
MASKED = -1e30
LANES = 128
SUBLANES = 8
V7X_VMEM_BYTES = 64 * 1024 * 1024
VMEM_LIMIT = V7X_VMEM_BYTES - 8 * 1024 * 1024
NEW_KEY_PAD = 128
Q_TILE = 256
AB_PAGES_PER_STEP = 32
MOBA_PAGES_PER_STEP = 32

def _cp(*sem):
    return pltpu.CompilerParams(dimension_semantics=sem, vmem_limit_bytes=VMEM_LIMIT)


def _pick(n, pref):
    if n <= pref:
        return n
    d = pref - pref % LANES
    while d >= LANES:
        if n % d == 0:
            return d
        d -= LANES
    return n


def _dot(a, b):
    return jnp.dot(a, b, preferred_element_type=F32)


def _dot_nt(a, b):
    return lax.dot_general(a, b, (((1,), (1,)), ((), ())), preferred_element_type=F32)


def _dot3_nt(a, b):
    ah = a.astype(BF16)
    al = (a - ah.astype(F32)).astype(BF16)
    bh = b.astype(BF16)
    bl = (b - bh.astype(F32)).astype(BF16)
    return _dot_nt(ah, bh) + (_dot_nt(ah, bl) + _dot_nt(al, bh))


def _rms(x):
    return x * lax.rsqrt(jnp.mean(x * x, axis=-1, keepdims=True) + EPS)


def _ada_kernel(c_ref, w_ref, b_ref, o_ref):
    c = c_ref[...]
    a = (c * (1.0 / (1.0 + jnp.exp(-c)))).astype(BF16)
    o_ref[0] = _dot(a, w_ref[0].astype(BF16)) + b_ref[0]


def _ada(c_pad, w_ada, b_ada):
    nl, d, n = w_ada.shape
    bc = c_pad.shape[0]
    bn = _pick(n, 512)
    return pl.pallas_call(
        _ada_kernel,
        out_shape=jax.ShapeDtypeStruct((nl, bc, n), F32),
        grid=(nl, n // bn),
        in_specs=[pl.BlockSpec((bc, d), lambda l, j: (0, 0)),
                  pl.BlockSpec((1, d, bn), lambda l, j: (l, 0, j)),
                  pl.BlockSpec((1, 1, bn), lambda l, j: (l, 0, j))],
        out_specs=pl.BlockSpec((1, bc, bn), lambda l, j: (l, 0, j)),
        compiler_params=_cp("arbitrary", "arbitrary"),
        name="ada",
    )(c_pad, w_ada, b_ada.reshape(nl, 1, n))


def _postpre_kernel(*refs, first_sample_blk, has_y, has_h, gate_idx, pre_idx):
    it = iter(refs)
    x_ref = next(it)
    if has_y:
        y_ref, mpp_ref, mps_ref, gpost_ref = next(it), next(it), next(it), next(it)
    if has_h:
        mhp_ref, mhs_ref, gpre_ref = next(it), next(it), next(it)
    if has_y:
        xo_ref = next(it)
    if has_h:
        h_ref = next(it)

    def run(mpost_ref, mpre_ref):
        x = x_ref[...]
        if has_y:
            gate = mpost_ref[:, gate_idx:gate_idx + 1, :]
            x = x + gate * (_rms(y_ref[...]) * gpost_ref[...])
            xo_ref[...] = x
        if has_h:
            shift = mpre_ref[:, pre_idx:pre_idx + 1, :]
            scale = mpre_ref[:, pre_idx + 1:pre_idx + 2, :]
            h_ref[...] = ((_rms(x) * gpre_ref[...]) * (1.0 + scale) + shift).astype(h_ref.dtype)

    i = pl.program_id(0)

    @pl.when(i < first_sample_blk)
    def _():
        run(mpp_ref if has_y else None, mhp_ref if has_h else None)

    @pl.when(i >= first_sample_blk)
    def _():
        run(mps_ref if has_y else None, mhs_ref if has_h else None)


def _postpre(x, y, mod_post, g_post, mod_pre, g_pre, *, dims, gate_idx, pre_idx, part="all"):
    b, t, db, ts = dims
    ng, _, d = x.shape
    gb = max(1, 256 // ts)
    while db % gb or (t // ts) % gb:
        gb //= 2
    bpb = (t // ts) // gb
    nblk_p = b * bpb
    nblk = ng // gb
    lo, hi = {"all": (0, nblk), "prompt": (0, nblk_p), "sample": (nblk_p, nblk)}[part]
    has_y, has_h = y is not None, mod_pre is not None

    xspec = pl.BlockSpec((gb, ts, d), lambda i: (i + lo, 0, 0))
    ospec = pl.BlockSpec((gb, ts, d), lambda i: (i, 0, 0))
    pspec = pl.BlockSpec((1, 6, d), lambda i: (jnp.minimum((i + lo) // bpb, b - 1), 0, 0))
    sspec = pl.BlockSpec((gb, 6, d), lambda i: (jnp.maximum(i + lo - nblk_p, 0), 0, 0))
    gspec = pl.BlockSpec((1, 1, d), lambda i: (0, 0, 0))
    args, specs = [x], [xspec]
    if has_y:
        args += [y, mod_post[0], mod_post[1], g_post.reshape(1, 1, d)]
        specs += [xspec, pspec, sspec, gspec]
    if has_h:
        args += [mod_pre[0], mod_pre[1], g_pre.reshape(1, 1, d)]
        specs += [pspec, sspec, gspec]
    oshape = ((hi - lo) * gb, ts, d)
    out_shape, out_specs = [], []
    if has_y:
        out_shape.append(jax.ShapeDtypeStruct(oshape, F32))
        out_specs.append(ospec)
    if has_h:
        out_shape.append(jax.ShapeDtypeStruct(oshape, BF16))
        out_specs.append(ospec)
    outs = pl.pallas_call(
        functools.partial(_postpre_kernel, first_sample_blk=nblk_p - lo, has_y=has_y, has_h=has_h,
                          gate_idx=gate_idx, pre_idx=pre_idx),
        out_shape=out_shape, grid=(hi - lo,), in_specs=specs, out_specs=out_specs,
        compiler_params=_cp("arbitrary"), name="postpre",
    )(*args)
    return outs


def _mm_kernel(*refs, nk, prologue, epilogue):
    a_ref, w_ref = refs[0], refs[1]
    idx = 2
    if prologue == "rms":
        g_ref = refs[idx]
        idx += 1
    o_ref = refs[idx]
    acc_ref = refs[idx + 1] if nk > 1 else None

    a = a_ref[...]
    if prologue == "rms":
        a = _rms(a.astype(F32)) * g_ref[...]
    w = w_ref[...].reshape(w_ref.shape[-2:]).astype(BF16)

    def finish(v):
        if epilogue == "relu2":
            r = jnp.maximum(v, 0.0)
            v = r * r
        o_ref[...] = v.astype(o_ref.dtype)

    part = _dot(a.astype(BF16), w)
    if nk == 1:
        finish(part)
    else:
        k = pl.program_id(2)

        @pl.when(k == 0)
        def _():
            acc_ref[...] = part

        @pl.when(jnp.logical_and(k > 0, k < nk - 1))
        def _():
            acc_ref[...] += part

        @pl.when(k == nk - 1)
        def _():
            finish(acc_ref[...] + part)


def _matmul(a, w, out_dtype, *, bm=1024, bn=512, bk=None, a_cols=None, gain=None, epilogue=None, layer=None):
    m = a.shape[0]
    k, n = w.shape[-2:]
    bm = _pick(m, bm)
    bn = min(bn, n)
    bk = k if bk is None else _pick(k, bk)
    nk = k // bk
    col0 = 0
    if a_cols is not None:
        assert a_cols[1] == k and a_cols[0] % bk == 0
        col0 = a_cols[0] // bk
    prologue = None
    args = [a, w]
    if layer is None:
        wspec = pl.BlockSpec((bk, bn), lambda i, j, kk: (kk, j))
    else:
        wspec = pl.BlockSpec((1, bk, bn), lambda i, j, kk: (layer, kk, j))
    specs = [pl.BlockSpec((bm, bk), lambda i, j, kk: (i, col0 + kk)), wspec]
    if gain is not None:
        assert nk == 1
        prologue = "rms"
        args.append(gain.reshape(1, k))
        specs.append(pl.BlockSpec((1, k), lambda i, j, kk: (0, 0)))
    return pl.pallas_call(
        functools.partial(_mm_kernel, nk=nk, prologue=prologue, epilogue=epilogue),
        out_shape=jax.ShapeDtypeStruct((m, n), out_dtype),
        grid=(m // bm, pl.cdiv(n, bn), nk),
        in_specs=specs,
        out_specs=pl.BlockSpec((bm, bn), lambda i, j, kk: (i, j)),
        scratch_shapes=[pltpu.VMEM((bm, bn), F32)] if nk > 1 else [],
        compiler_params=_cp("arbitrary", "arbitrary", "arbitrary"),
        name="matmul",
    )(*args)


def _rope128(x, cosf, sins):
    lane = lax.broadcasted_iota(jnp.int32, x.shape, 1)
    first = (lane & 63) < 32
    swapped = jnp.where(first, pltpu.roll(x, 96, 1), pltpu.roll(x, 32, 1))
    return x * cosf + swapped * sins


def _qhead_kernel(qn_ref, qr_ref, wuk_ref, cos_ref, sin_ref, ql_ref, qp_ref, *, qscale):
    ql = _dot(qn_ref[...].astype(BF16), wuk_ref[0].astype(BF16))
    ql_ref[...] = (ql * qscale).astype(ql_ref.dtype)
    qp_ref[...] = (_rope128(qr_ref[...], cos_ref[...], sin_ref[...]) * qscale).astype(qp_ref.dtype)


def _qhead(q, wuk_t, cosf, sins, *, heads, qscale):
    m = q.shape[0]
    _, nope, rank = wuk_t.shape
    bm = _pick(m, 1024)
    return pl.pallas_call(
        functools.partial(_qhead_kernel, qscale=qscale),
        out_shape=[jax.ShapeDtypeStruct((m, heads * rank), BF16),
                   jax.ShapeDtypeStruct((m, heads * LANES), BF16)],
        grid=(m // bm, heads),
        in_specs=[pl.BlockSpec((bm, nope), lambda i, h: (i, h)),
                  pl.BlockSpec((bm, LANES), lambda i, h: (i, heads + h)),
                  pl.BlockSpec((1, nope, rank), lambda i, h: (h, 0, 0)),
                  pl.BlockSpec((bm, LANES), lambda i, h: (i, 0)),
                  pl.BlockSpec((bm, LANES), lambda i, h: (i, 0))],
        out_specs=[pl.BlockSpec((bm, rank), lambda i, h: (i, h)),
                   pl.BlockSpec((bm, LANES), lambda i, h: (i, h))],
        compiler_params=_cp("arbitrary", "arbitrary"),
        name="qhead",
    )(q, q, wuk_t, cosf, sins)


def _headmm_kernel(a_ref, w_ref, o_ref):
    o_ref[...] = _dot(a_ref[...].astype(BF16), w_ref[0].astype(BF16)).astype(o_ref.dtype)


def _headmm(a, w, out_dtype):
    m = a.shape[0]
    heads, k, n = w.shape
    bm = _pick(m, 1024)
    return pl.pallas_call(
        _headmm_kernel,
        out_shape=jax.ShapeDtypeStruct((m, heads * n), out_dtype),
        grid=(m // bm, heads),
        in_specs=[pl.BlockSpec((bm, k), lambda i, h: (i, h)),
                  pl.BlockSpec((1, k, n), lambda i, h: (h, 0, 0))],
        out_specs=pl.BlockSpec((bm, n), lambda i, h: (i, h)),
        compiler_params=_cp("arbitrary", "arbitrary"),
        name="headmm",
    )(a, w)


def _kvrows_kernel(z_ref, g_ref, cos_ref, sin_ref, of_ref, ob_ref, *, rank):
    z = z_ref[...]
    lat = _rms(z[:, :rank]) * g_ref[...]
    pe = _rope128(z[:, rank:], cos_ref[...], sin_ref[...])
    of_ref[:, :rank] = lat
    of_ref[:, rank:] = pe
    ob_ref[:, :rank] = lat.astype(BF16)
    ob_ref[:, rank:] = pe.astype(BF16)


def _kvrows(z, col_blk, g_kv, cosf, sins, *, rank):
    m = z.shape[0]
    w = rank + LANES
    bm = _pick(m, 1024)
    return pl.pallas_call(
        functools.partial(_kvrows_kernel, rank=rank),
        out_shape=[jax.ShapeDtypeStruct((m, w), F32), jax.ShapeDtypeStruct((m, w), BF16)],
        grid=(m // bm,),
        in_specs=[pl.BlockSpec((bm, w), lambda i: (i, col_blk)),
                  pl.BlockSpec((1, rank), lambda i: (0, 0)),
                  pl.BlockSpec((bm, LANES), lambda i: (i, 0)),
                  pl.BlockSpec((bm, LANES), lambda i: (i, 0))],
        out_specs=[pl.BlockSpec((bm, w), lambda i: (i, 0)), pl.BlockSpec((bm, w), lambda i: (i, 0))],
        compiler_params=_cp("arbitrary"),
        name="kvrows",
    )(z, g_kv.reshape(1, rank), cosf, sins)


def _flash_update(s, rc, v, m_ref, l_ref, acc_ref, rows=slice(None), v_transposed=False):
    m_old = m_ref[rows]
    m_new = jnp.maximum(m_old, jnp.max(s, axis=-1, keepdims=True) + rc)
    alpha = jnp.exp2(m_old - m_new)
    p = jnp.exp2(s - (m_new - rc))
    l_ref[rows] = alpha * l_ref[rows] + jnp.sum(p, axis=-1, keepdims=True)
    pb = p.astype(BF16)
    acc_ref[rows] = alpha * acc_ref[rows] + (_dot_nt(pb, v) if v_transposed else _dot(pb, v))
    m_ref[rows] = m_new


def _flash_init(m_ref, l_ref, acc_ref):
    m_ref[...] = jnp.full(m_ref.shape, NEG_INF, F32)
    l_ref[...] = jnp.zeros(l_ref.shape, F32)
    acc_ref[...] = jnp.zeros(acc_ref.shape, F32)


def _lambda_value(lam_ref):
    lv = lam_ref[...]
    d1 = jnp.sum(lv[0:1] * lv[1:2], axis=-1, keepdims=True)
    d2 = jnp.sum(lv[2:3] * lv[3:4], axis=-1, keepdims=True)
    return jnp.exp(d1) - jnp.exp(d2) + DIFF_LAMBDA_INIT


def _split_halves(q):
    lane = lax.broadcasted_iota(jnp.int32, q.shape, 1)
    lo = lane < (q.shape[1] // 2)
    zero = jnp.zeros_like(q)
    return jnp.where(lo, q, zero), jnp.where(lo, zero, q)


def _diff_finish(acc1, l1, acc2, l2, lam, g):
    a = acc1 / l1 - lam * (acc2 / l2)
    return (_rms(a) * g) * (1.0 - DIFF_LAMBDA_INIT)


def _flash_update_t(st, rc, vt, m_ref, l_ref, acc_ref, cols=slice(None)):
    m_old = m_ref[:, cols]
    m_new = jnp.maximum(m_old, jnp.max(st, axis=0, keepdims=True) + rc)
    alpha = jnp.exp2(m_old - m_new)
    p = jnp.exp2(st - (m_new - rc))
    l_ref[:, cols] = alpha * l_ref[:, cols] + jnp.sum(p, axis=0, keepdims=True)
    acc_ref[:, cols] = alpha * acc_ref[:, cols] + _dot(vt, p.astype(BF16))
    m_ref[:, cols] = m_new


def _causal_keep_t(tq, cols):
    key = lax.broadcasted_iota(jnp.int32, (tq, cols), 0)
    pos = lax.broadcasted_iota(jnp.int32, (tq, cols), 1) & (tq - 1)
    return key <= pos


def _rel_bias_t(tq):
    key = lax.broadcasted_iota(jnp.int32, (tq, tq), 0)
    pos = lax.broadcasted_iota(jnp.int32, (tq, tq), 1)
    return (key - pos).astype(F32)


def _transpose_bf16(x):
    return x.astype(F32).T.astype(BF16)


def _mla_prompt_kernel(ql_ref, qp_ref, k_ref, o_ref, qs, kt, m_ref, l_ref, acc_ref, *, tq, nq, heads, rank):
    qi = pl.program_id(1)

    @pl.when(qi == 0)
    def _():
        for c in range(nq):
            kt[c] = _transpose_bf16(k_ref[c * tq:(c + 1) * tq, :rank])

    for h in range(heads):
        rows = slice(h * tq, (h + 1) * tq)
        qs[rows, :rank] = ql_ref[:, h * rank:(h + 1) * rank]
        qs[rows, rank:] = qp_ref[:, h * LANES:(h + 1) * LANES]
    _flash_init(m_ref, l_ref, acc_ref)

    def body(c, carry):
        kc = k_ref[pl.ds(pl.multiple_of(c * tq, tq), tq), :]
        _flash_update_t(_dot_nt(kc, qs[...]), 0.0, kt[c], m_ref, l_ref, acc_ref)
        return carry

    lax.fori_loop(0, qi, body, 0)
    kc = k_ref[pl.ds(pl.multiple_of(qi * tq, tq), tq), :]
    st = jnp.where(_causal_keep_t(tq, heads * tq), _dot_nt(kc, qs[...]), NEG_INF)
    _flash_update_t(st, 0.0, kt[qi], m_ref, l_ref, acc_ref)
    for h in range(heads):
        cols = slice(h * tq, (h + 1) * tq)
        o_ref[:, h * rank:(h + 1) * rank] = (acc_ref[:, cols] / l_ref[:, cols]).T.astype(o_ref.dtype)


def _mla_prompt(q_lat, q_pe, rows_bf, *, b, t, heads, rank):
    tq = Q_TILE
    assert t % tq == 0
    nq = t // tq
    w = rank + LANES
    r = heads * tq
    return pl.pallas_call(
        functools.partial(_mla_prompt_kernel, tq=tq, nq=nq, heads=heads, rank=rank),
        out_shape=jax.ShapeDtypeStruct((b * t, heads * rank), BF16),
        grid=(b, nq),
        in_specs=[pl.BlockSpec((tq, heads * rank), lambda bb, i: (bb * nq + i, 0)),
                  pl.BlockSpec((tq, heads * LANES), lambda bb, i: (bb * nq + i, 0)),
                  pl.BlockSpec((t, w), lambda bb, i: (bb, 0))],
        out_specs=pl.BlockSpec((tq, heads * rank), lambda bb, i: (bb * nq + i, 0)),
        scratch_shapes=[pltpu.VMEM((r, w), BF16), pltpu.VMEM((nq, rank, tq), BF16),
                        pltpu.VMEM((1, r), F32), pltpu.VMEM((1, r), F32), pltpu.VMEM((rank, r), F32)],
        compiler_params=_cp("arbitrary", "arbitrary"),
        name="mla_prompt",
    )(q_lat, q_pe, rows_bf)


def _diff_prompt_kernel(slopes_ref, q_ref, k_ref, v_ref, lam_ref, g_ref, o_ref,
                        qs, kb, vt, relb, slr, m_ref, l_ref, acc_ref, *, tq, nq, heads, dv, qscale):
    qi = pl.program_id(1)
    hr = heads * tq

    @pl.when(qi == 0)
    def _():
        kb[...] = k_ref[...].astype(BF16)
        for c in range(nq):
            vt[c] = v_ref[c * tq:(c + 1) * tq, :].T.astype(BF16)
        rel = _rel_bias_t(tq)
        for h in range(heads):
            s2 = slopes_ref[h] * LOG2E
            relb[:, h * tq:(h + 1) * tq] = s2 * rel
            slr[:, h * tq:(h + 1) * tq] = jnp.full((1, tq), s2, F32)

    for h in range(heads):
        q1, q2 = _split_halves((q_ref[:, h * LANES:(h + 1) * LANES] * qscale).astype(BF16))
        qs[h * tq:(h + 1) * tq, :] = q1
        qs[hr + h * tq:hr + (h + 1) * tq, :] = q2
    _flash_init(m_ref, l_ref, acc_ref)
    halves = (slice(0, hr), slice(hr, 2 * hr))

    def body(c, carry):
        k = kb[pl.ds(pl.multiple_of(c * tq, tq), tq), :]
        rc = slr[...] * ((c - qi) * tq).astype(F32)
        for half in halves:
            _flash_update_t(_dot_nt(k, qs[half]) + relb[...], rc, vt[c], m_ref, l_ref, acc_ref, half)
        return carry

    lax.fori_loop(0, qi, body, 0)
    k = kb[pl.ds(pl.multiple_of(qi * tq, tq), tq), :]
    keep = _causal_keep_t(tq, hr)
    for half in halves:
        st = jnp.where(keep, _dot_nt(k, qs[half]) + relb[...], NEG_INF)
        _flash_update_t(st, 0.0, vt[qi], m_ref, l_ref, acc_ref, half)

    lam = _lambda_value(lam_ref)
    for h in range(heads):
        c1 = slice(h * tq, (h + 1) * tq)
        c2 = slice(hr + h * tq, hr + (h + 1) * tq)
        a = acc_ref[:, c1] / l_ref[:, c1] - lam * (acc_ref[:, c2] / l_ref[:, c2])
        o_ref[:, h * dv:(h + 1) * dv] = (
            (_rms(a.T) * g_ref[...]) * (1.0 - DIFF_LAMBDA_INIT)).astype(o_ref.dtype)


def _diff_prompt(slopes, z0, lam_in, g_sub, *, b, t, heads, k_blk, v_blk, dv, qscale):
    tq = Q_TILE
    assert t % tq == 0
    nq = t // tq
    hr = heads * tq
    return pl.pallas_call(
        functools.partial(_diff_prompt_kernel, tq=tq, nq=nq, heads=heads, dv=dv, qscale=qscale),
        out_shape=jax.ShapeDtypeStruct((b * t, heads * dv), BF16),
        grid=(b, nq),
        in_specs=[pl.BlockSpec(memory_space=pltpu.SMEM),
                  pl.BlockSpec((tq, heads * LANES), lambda bb, i: (bb * nq + i, 0)),
                  pl.BlockSpec((t, LANES), lambda bb, i: (bb, k_blk)),
                  pl.BlockSpec((t, dv), lambda bb, i: (bb, v_blk)),
                  pl.BlockSpec((8, LANES), lambda bb, i: (0, 0)),
                  pl.BlockSpec((1, dv), lambda bb, i: (0, 0))],
        out_specs=pl.BlockSpec((tq, heads * dv), lambda bb, i: (bb * nq + i, 0)),
        scratch_shapes=[pltpu.VMEM((2 * hr, LANES), BF16), pltpu.VMEM((t, LANES), BF16),
                        pltpu.VMEM((nq, dv, tq), BF16), pltpu.VMEM((tq, hr), F32), pltpu.VMEM((1, hr), F32),
                        pltpu.VMEM((1, 2 * hr), F32), pltpu.VMEM((1, 2 * hr), F32),
                        pltpu.VMEM((dv, 2 * hr), F32)],
        compiler_params=_cp("arbitrary", "arbitrary"),
        name="diff_prompt",
    )(slopes, z0, z0, z0, lam_in, g_sub.reshape(1, dv))


def _ab_sample_kernel(pt_ref, ql_ref, qp_ref, qd_ref, kn_ref, dkn_ref, dvn_ref, slope_ref, t_ref,
                      tab_ref, lam_ref, g_ref, *rest, npg, page, rank, rope, past, nchunk, qscale_d):
    mla_pg = rest[:npg]
    dk_pg = rest[npg:2 * npg]
    dv_pg = rest[2 * npg:3 * npg]
    ol_ref, od_ref = rest[3 * npg:3 * npg + 2]
    kt, dkb, dvb, m_a, l_a, acc_a, m_d, l_d, acc_d = rest[3 * npg + 2:]
    j = pl.program_id(1)
    kc = npg * page
    row_w = rank + rope

    @pl.when(j == 0)
    def _():
        _flash_init(m_a, l_a, acc_a)
        _flash_init(m_d, l_d, acc_d)
        kt[row_w:, :] = jnp.zeros((kt.shape[0] - row_w, kc), BF16)

    for p in range(npg):
        cols = slice(p * page, (p + 1) * page)
        kt[:row_w, cols] = mla_pg[p][0].astype(BF16)
        dkb[cols, :] = dk_pg[p][0].astype(BF16)
        dvb[cols, :] = dv_pg[p][0].astype(BF16)

    ql = ql_ref[0]
    qp = qp_ref[0]
    q1, q2 = _split_halves((qd_ref[0] * qscale_d).astype(BF16))
    qd = jnp.concatenate([q1, q2], axis=0)
    slope2 = slope_ref[...] * LOG2E
    tq = t_ref[...]
    r = ql.shape[0]

    rc = slope2 * ((j * kc - past).astype(F32) - tq)
    lat_t = kt[:rank, :]
    _flash_update(_dot(ql, lat_t) + _dot(qp, kt[rank:, :]), 0.0, lat_t, m_a, l_a, acc_a, v_transposed=True)
    _flash_update(_dot_nt(qd, dkb[...]) + tab_ref[...], rc, dvb[...], m_d, l_d, acc_d)

    @pl.when(j == nchunk - 1)
    def _():
        kn = kn_ref[0]
        nk = kn.shape[0]
        tk = lax.broadcasted_iota(jnp.int32, (2 * r, nk), 1).astype(F32)
        keep = tk <= tq
        latn = kn[:, :rank]
        sn = _dot_nt(ql, latn) + _dot_nt(qp, kn[:, rank:])
        _flash_update(jnp.where(keep[:r], sn, NEG_INF), 0.0, latn, m_a, l_a, acc_a)
        sd = _dot_nt(qd, dkn_ref[0]) + slope2 * (tk - tq)
        _flash_update(jnp.where(keep, sd, NEG_INF), 0.0, dvn_ref[0], m_d, l_d, acc_d)
        ol_ref[0] = (acc_a[...] / l_a[...]).astype(ol_ref.dtype)
        od_ref[0] = _diff_finish(acc_d[:r], l_d[:r], acc_d[r:], l_d[r:],
                                 _lambda_value(lam_ref), g_ref[...]).astype(od_ref.dtype)


def _pages_per_step(n_pages, most):
    for p in (32, 16, 8, 4, 2, 1):
        if p > most:
            continue
        if n_pages % p == 0 and n_pages // p >= 2:
            return p
    return 1


def _ab_sample(pt_flat, ql, qp, qd, kn, dkn, dvn, slope_col, t_col, lam_in, g_sub,
               cache_mla, cache_dk, cache_dv, *, n_pages, rank, rope, qscale_d):
    db, r, _ = ql.shape
    row_w, page = cache_mla.shape[1:]
    dv = cache_dv.shape[2]
    npg = _pages_per_step(n_pages, AB_PAGES_PER_STEP)
    nchunk = n_pages // npg
    kc = npg * page
    past = n_pages * page
    slope2 = jnp.concatenate([slope_col, slope_col], axis=0)
    t2 = jnp.concatenate([t_col, t_col], axis=0)
    tab = (slope2 * LOG2E) * jnp.arange(kc, dtype=F32)[None, :]

    def page_spec(shape, p):
        return pl.BlockSpec((1,) + shape, lambda b, j, pt: (pt[b * n_pages + j * npg + p], 0, 0))

    def seq_spec(shape):
        return pl.BlockSpec((1,) + shape, lambda b, j, pt: (b, 0, 0))

    def const_spec(shape):
        return pl.BlockSpec(shape, lambda b, j, pt: (0, 0))

    in_specs = [seq_spec((r, rank)), seq_spec((r, LANES)), seq_spec((r, LANES)),
                seq_spec((NEW_KEY_PAD, rank + LANES)), seq_spec((NEW_KEY_PAD, LANES)),
                seq_spec((NEW_KEY_PAD, dv)),
                const_spec((2 * r, 1)), const_spec((2 * r, 1)),
                pl.BlockSpec((2 * r, kc), lambda b, j, pt: (0, 0), pipeline_mode=pl.Buffered(1)),
                const_spec((8, LANES)), const_spec((1, dv))]
    in_specs += [page_spec((row_w, page), p) for p in range(npg)]
    in_specs += [page_spec((page, LANES), p) for p in range(npg)]
    in_specs += [page_spec((page, dv), p) for p in range(npg)]
    scratch = [pltpu.VMEM((rank + LANES, kc), BF16),
               pltpu.VMEM((kc, LANES), BF16), pltpu.VMEM((kc, dv), BF16),
               pltpu.VMEM((r, 1), F32), pltpu.VMEM((r, 1), F32), pltpu.VMEM((r, rank), F32),
               pltpu.VMEM((2 * r, 1), F32), pltpu.VMEM((2 * r, 1), F32), pltpu.VMEM((2 * r, dv), F32)]
    return pl.pallas_call(
        functools.partial(_ab_sample_kernel, npg=npg, page=page, rank=rank, rope=rope, past=past,
                          nchunk=nchunk, qscale_d=qscale_d),
        out_shape=[jax.ShapeDtypeStruct((db, r, rank), BF16), jax.ShapeDtypeStruct((db, r, dv), BF16)],
        grid_spec=pltpu.PrefetchScalarGridSpec(
            num_scalar_prefetch=1, grid=(db, nchunk), in_specs=in_specs,
            out_specs=[seq_spec((r, rank)), seq_spec((r, dv))], scratch_shapes=scratch),
        compiler_params=_cp("arbitrary", "arbitrary"),
        name="ab_sample",
    )(pt_flat, ql, qp, qd, kn, dkn, dvn, slope2, t2, tab, lam_in, g_sub.reshape(1, dv),
      *([cache_mla] * npg), *([cache_dk] * npg), *([cache_dv] * npg))


def _topk_mask_t(gate_ref, nblk, n_valid, n_sel, always):
    gate = gate_ref[...]
    blk = lax.broadcasted_iota(jnp.int32, gate.shape, 0)

    def body(k, rank):
        gk = gate_ref[pl.ds(k, 1), :]
        beats = jnp.logical_or(gk > gate, jnp.logical_and(gk == gate, k < blk))
        return rank + jnp.where(jnp.logical_and(beats, k < n_valid), 1.0, 0.0)

    rank = lax.fori_loop(0, nblk, body, jnp.zeros(gate.shape, F32), unroll=8 if nblk % 8 == 0 else 1)
    chosen = jnp.logical_and(blk < n_valid, rank < float(n_sel))
    return jnp.where(jnp.logical_or(chosen, blk == always), 0.0, MASKED)


def _moba_prompt_kernel(slopes_ref, q_ref, k_ref, v_ref, o_ref,
                        qs, kaug, vt, kmean, relb, slr, gsc, m_ref, l_ref, acc_ref,
                        *, blk, nblk, rh, d, qscale):
    g = pl.program_id(1)
    qi = pl.program_id(2)
    hr = rh * blk

    @pl.when(qi == 0)
    def _():
        t = k_ref.shape[0]
        kaug[:, :d] = k_ref[...].astype(BF16)
        krow_blk = lax.broadcasted_iota(jnp.int32, (t, d), 0) >> BLOCK_SHIFT
        lane = lax.broadcasted_iota(jnp.int32, (t, d), 1)
        kaug[:, d:] = jnp.where((lane & 7) == krow_blk, 1.0, 0.0).astype(BF16)
        kmean[...] = jnp.zeros(kmean.shape, F32)
        for n in range(nblk):
            vt[n] = v_ref[n * blk:(n + 1) * blk, :].T.astype(BF16)
            kmean[n:n + 1, :] = jnp.mean(k_ref[n * blk:(n + 1) * blk, :], axis=0, keepdims=True)
        rel = _rel_bias_t(blk)
        for h in range(rh):
            s2 = slopes_ref[g * rh + h] * LOG2E
            relb[:, h * blk:(h + 1) * blk] = s2 * rel
            slr[:, h * blk:(h + 1) * blk] = jnp.full((1, blk), s2, F32)

    km = kmean[...]
    for h in range(rh):
        qh = q_ref[:, h * d:(h + 1) * d]
        qs[h * blk:(h + 1) * blk, :d] = (qh * qscale).astype(BF16)
        gsc[h * SUBLANES:(h + 1) * SUBLANES, :] = _dot3_nt(km, qh)

    gate = gsc[...].reshape(rh, SUBLANES, blk)
    bidx = lax.broadcasted_iota(jnp.int32, gate.shape, 1)
    rank = jnp.zeros(gate.shape, F32)
    for k in range(nblk):
        gk = gate[:, k:k + 1, :]
        beats = jnp.logical_or(gk > gate, jnp.logical_and(gk == gate, k < bidx))
        rank = rank + jnp.where(jnp.logical_and(beats, k < qi), 1.0, 0.0)
    chosen = jnp.logical_and(bidx < qi, rank < float(min(MOBA_TOPK, nblk - 1)))
    sel = jnp.where(jnp.logical_or(chosen, bidx == qi), 0.0, MASKED)
    sel_rows = sel.reshape(rh * SUBLANES, blk).T.astype(BF16)
    lane_head = lax.broadcasted_iota(jnp.int32, sel_rows.shape, 1) >> 3
    zero = jnp.zeros_like(sel_rows)
    for h in range(rh):
        qs[h * blk:(h + 1) * blk, d:] = jnp.where(lane_head == h, sel_rows, zero)

    _flash_init(m_ref, l_ref, acc_ref)
    own = pl.multiple_of(qi * blk, blk)
    st = jnp.where(_causal_keep_t(blk, hr), _dot_nt(kaug[pl.ds(own, blk), :], qs[...]) + relb[...], NEG_INF)
    _flash_update_t(st, 0.0, vt[qi], m_ref, l_ref, acc_ref)

    def body(jb, carry):
        off = pl.multiple_of(jb * blk, blk)
        rc = slr[...] * ((jb - qi) * blk).astype(F32)
        _flash_update_t(_dot_nt(kaug[pl.ds(off, blk), :], qs[...]) + relb[...], rc, vt[jb],
                        m_ref, l_ref, acc_ref)
        return carry

    lax.fori_loop(0, qi, body, 0)
    for h in range(rh):
        cols = slice(h * blk, (h + 1) * blk)
        o_ref[:, h * d:(h + 1) * d] = (acc_ref[:, cols] / l_ref[:, cols]).T.astype(o_ref.dtype)


def _moba_prompt(slopes, zc, *, b, t, heads, groups, d, k_blk, v_blk, qscale):
    blk = MOBA_BLOCK
    assert t % blk == 0
    nblk = t // blk
    rh = heads // groups
    assert nblk <= SUBLANES and rh * SUBLANES == LANES and d == LANES
    hr = rh * blk
    return pl.pallas_call(
        functools.partial(_moba_prompt_kernel, blk=blk, nblk=nblk, rh=rh, d=d, qscale=qscale),
        out_shape=jax.ShapeDtypeStruct((b * t, heads * d), BF16),
        grid=(b, groups, nblk),
        in_specs=[pl.BlockSpec(memory_space=pltpu.SMEM),
                  pl.BlockSpec((blk, rh * d), lambda bb, g, i: (bb * nblk + i, g)),
                  pl.BlockSpec((t, d), lambda bb, g, i: (bb, k_blk + g)),
                  pl.BlockSpec((t, d), lambda bb, g, i: (bb, v_blk + g))],
        out_specs=pl.BlockSpec((blk, rh * d), lambda bb, g, i: (bb * nblk + i, g)),
        scratch_shapes=[pltpu.VMEM((hr, 2 * d), BF16), pltpu.VMEM((t, 2 * d), BF16),
                        pltpu.VMEM((nblk, d, blk), BF16),
                        pltpu.VMEM((SUBLANES, d), F32), pltpu.VMEM((blk, hr), F32), pltpu.VMEM((1, hr), F32),
                        pltpu.VMEM((rh * SUBLANES, blk), F32),
                        pltpu.VMEM((1, hr), F32), pltpu.VMEM((1, hr), F32), pltpu.VMEM((d, hr), F32)],
        compiler_params=_cp("arbitrary", "arbitrary", "arbitrary"),
        name="moba_prompt",
    )(slopes, zc, zc, zc)


def _moba_sample_kernel(pt_ref, q_ref, kn_ref, vn_ref, slope_ref, t_ref, tab_ref, *rest,
                        npg, page, groups, d, past, nchunk, nfull, qscale):
    k_pg = rest[:npg]
    v_pg = rest[npg:2 * npg]
    o_ref = rest[2 * npg]
    s_all, kb, vb, ksum, gsc, selb, m_s, l_s, acc_s = rest[2 * npg + 1:]
    j = pl.program_id(1)
    kc = npg * page
    ppb = MOBA_BLOCK // page
    bpc = kc // MOBA_BLOCK
    r = q_ref.shape[2]
    tq = t_ref[...]

    @pl.when(j == 0)
    def _():
        ksum[...] = jnp.zeros(ksum.shape, F32)

    @pl.when(j < nchunk)
    def _():
        for g in range(groups):
            csum = None
            for p in range(npg):
                x = k_pg[p][0, pl.ds(g, page, stride=groups), :]
                kb[p * page:(p + 1) * page, g * d:(g + 1) * d] = x.astype(BF16)
                cs = jnp.sum(x, axis=0, keepdims=True)
                csum = cs if p % ppb == 0 else csum + cs
                if p % ppb == ppb - 1:
                    ksum[g, pl.ds(j * bpc + p // ppb, 1), :] = csum
        for g in range(groups):
            qb = (q_ref[0, g] * qscale).astype(BF16)
            s_all[g, j] = _dot_nt(qb, kb[:, g * d:(g + 1) * d]) + tab_ref[g]

    @pl.when(j == nchunk - 1)
    def _():
        kn = kn_ref[0]
        vn = vn_ref[0]
        nk = kn.shape[0]
        tk = lax.broadcasted_iota(jnp.int32, (r, nk), 1).astype(F32)
        keep = tk <= tq
        for g in range(groups):
            q = q_ref[0, g]
            nb = gsc.shape[0]
            gsc[...] = _dot3_nt(ksum[g, :nb] * (1.0 / MOBA_BLOCK), q)
            mask = _topk_mask_t(gsc, nfull, nfull, min(MOBA_TOPK, nfull), -1)
            if nb < LANES:
                mask = jnp.concatenate([mask, jnp.zeros((LANES - nb, mask.shape[1]), F32)], axis=0)
            selb[g] = mask.T.astype(BF16)
            m_ref, l_ref, acc_ref = m_s.at[g], l_s.at[g], acc_s.at[g]
            _flash_init(m_ref, l_ref, acc_ref)
            sn = _dot_nt((q * qscale).astype(BF16), kn[:, g * d:(g + 1) * d]) + (slope_ref[g] * LOG2E) * (tk - tq)
            _flash_update(jnp.where(keep, sn, NEG_INF), 0.0, vn[:, g * d:(g + 1) * d], m_ref, l_ref, acc_ref)

    @pl.when(j >= nchunk)
    def _():
        c = j - nchunk
        for g in range(groups):
            for p in range(npg):
                vb[p * page:(p + 1) * page, g * d:(g + 1) * d] = (
                    v_pg[p][0, pl.ds(g, page, stride=groups), :].astype(BF16))
        kblk = c * bpc + (lax.broadcasted_iota(jnp.int32, (LANES, kc), 1) >> BLOCK_SHIFT)
        ind = jnp.where(lax.broadcasted_iota(jnp.int32, (LANES, kc), 0) == kblk, 1.0, 0.0).astype(BF16)
        for g in range(groups):
            rc = (slope_ref[g] * LOG2E) * ((c * kc - past).astype(F32) - tq)
            _flash_update(s_all[g, c] + _dot(selb[g], ind), rc, vb[:, g * d:(g + 1) * d],
                          m_s.at[g], l_s.at[g], acc_s.at[g])

    @pl.when(j == 2 * nchunk - 1)
    def _():
        for g in range(groups):
            o_ref[0, g] = (acc_s[g] / l_s[g]).astype(o_ref.dtype)


def _moba_sample(pt_flat, q, kn, vn, slope_col, t_col, cache_k, cache_v, *, n_pages, groups, d, qscale):
    db, _, r, _ = q.shape
    page = cache_k.shape[1] // groups
    assert MOBA_BLOCK % page == 0
    past = n_pages * page
    assert past % MOBA_BLOCK == 0
    nfull = past // MOBA_BLOCK
    assert 1 <= nfull <= LANES
    npg = _pages_per_step(n_pages, MOBA_PAGES_PER_STEP)
    while (npg * page) % MOBA_BLOCK:
        npg *= 2
    nchunk = n_pages // npg
    kc = npg * page
    gd = groups * d
    tab = (slope_col * LOG2E) * jnp.arange(kc, dtype=F32)[None, None, :]

    def kspec(p):
        return pl.BlockSpec((1, page * groups, d), lambda b, j, pt:
                            (pt[b * n_pages + jnp.minimum(j, nchunk - 1) * npg + p], 0, 0))

    def vspec(p):
        return pl.BlockSpec((1, page * groups, d), lambda b, j, pt:
                            (pt[b * n_pages + jnp.maximum(j - nchunk, 0) * npg + p], 0, 0))

    in_specs = [pl.BlockSpec((1, groups, r, d), lambda b, j, pt: (b, 0, 0, 0)),
                pl.BlockSpec((1, NEW_KEY_PAD, gd), lambda b, j, pt: (b, 0, 0)),
                pl.BlockSpec((1, NEW_KEY_PAD, gd), lambda b, j, pt: (b, 0, 0)),
                pl.BlockSpec((groups, r, 1), lambda b, j, pt: (0, 0, 0)),
                pl.BlockSpec((r, 1), lambda b, j, pt: (0, 0)),
                pl.BlockSpec((groups, r, kc), lambda b, j, pt: (0, 0, 0), pipeline_mode=pl.Buffered(1))]
    in_specs += [kspec(p) for p in range(npg)] + [vspec(p) for p in range(npg)]
    scratch = [pltpu.VMEM((groups, nchunk, r, kc), F32),
               pltpu.VMEM((kc, gd), BF16), pltpu.VMEM((kc, gd), BF16),
               pltpu.VMEM((groups, LANES, d), F32), pltpu.VMEM((-(-nfull // SUBLANES) * SUBLANES, r), F32),
               pltpu.VMEM((groups, r, LANES), BF16),
               pltpu.VMEM((groups, r, 1), F32), pltpu.VMEM((groups, r, 1), F32),
               pltpu.VMEM((groups, r, d), F32)]
    return pl.pallas_call(
        functools.partial(_moba_sample_kernel, npg=npg, page=page, groups=groups, d=d, past=past,
                          nchunk=nchunk, nfull=nfull, qscale=qscale),
        out_shape=jax.ShapeDtypeStruct((db, groups, r, d), BF16),
        grid_spec=pltpu.PrefetchScalarGridSpec(
            num_scalar_prefetch=1, grid=(db, 2 * nchunk), in_specs=in_specs,
            out_specs=pl.BlockSpec((1, groups, r, d), lambda b, j, pt: (b, 0, 0, 0)),
            scratch_shapes=scratch),
        compiler_params=_cp("arbitrary", "arbitrary"),
        name="moba_sample",
    )(pt_flat, q, kn, vn, slope_col, t_col, tab, *([cache_k] * npg), *([cache_v] * npg))


def _alibi_slopes(n):
    return np.asarray(2.0 ** (-8.0 * np.arange(1, n + 1) / n), dtype=np.float32)


def _rope_tables(pos, half):
    inv = ROPE_THETA ** (-jnp.arange(half, dtype=F32) / half)
    ang = pos.astype(F32)[:, None] * inv
    cos, sin = jnp.cos(ang), jnp.sin(ang)
    reps = LANES // (2 * half)
    return jnp.tile(cos, (1, 2 * reps)), jnp.tile(jnp.concatenate([-sin, sin], axis=-1), (1, reps))


def _pad_new(x, db, ts):
    x = x.reshape(db, ts, x.shape[-1]).astype(BF16)
    return jnp.pad(x, ((0, 0), (0, NEW_KEY_PAD - ts), (0, 0)))


def kernel(x_prompt, x_sample, cache_mla, cache_diff_k, cache_diff_v, cache_moba_k, cache_moba_v,
           page_table, c_prompt, c_sample, w_ada, b_ada, g_pre_mix, g_post_mix, g_pre_ffn, g_post_ffn,
           w_in_ab, g_q_a, w_q_b, g_kv_a, w_uk, w_uv, lambda_q1, lambda_k1, lambda_q2, lambda_k2,
           g_diff_sub, w_out_ab, w_in_c, w_out_c, w_ff_up, w_ff_down):
    b, t, d_model = x_prompt.shape
    db, ts, _ = x_sample.shape
    mp, ms = b * t, db * ts
    m = mp + ms
    dims = (b, t, db, ts)
    n_pool, page, row_w = cache_mla.shape
    n_pages = page_table.shape[1]
    past = n_pages * page
    q_rank = g_q_a.shape[0]
    kv_rank, heads_a, nope = w_uk.shape
    rope = row_w - kv_rank
    dh = lambda_q1.shape[0]
    dv = g_diff_sub.shape[0]
    heads_b = (w_in_ab.shape[1] - q_rank - row_w - 2 * dh - dv) // (2 * dh)
    groups, d_c = cache_moba_k.shape[2], cache_moba_k.shape[3]
    heads_c = w_out_c.shape[0] // d_c
    rh = heads_c // groups
    assert nope == LANES and 2 * rope == LANES and 2 * dh == LANES and dv == LANES and d_c == LANES
    assert kv_rank % LANES == 0 and heads_a == heads_b and cache_diff_k.shape[2] == 1
    assert ts <= NEW_KEY_PAD

    o1, o2 = q_rank, q_rank + row_w
    o3 = o2 + heads_b * 2 * dh
    w_kv = jnp.pad(w_in_ab[:, o1:o2], ((0, 0), (0, LANES - rope)))
    w0 = jnp.concatenate([w_in_ab[:, o2:o3], w_in_ab[:, :o1], w_kv, w_in_ab[:, o3:]], axis=1)
    c_qa = heads_b * 2 * dh
    c_kv = c_qa + q_rank
    c_dk = c_kv + kv_rank + LANES
    c_dv = c_dk + 2 * dh
    assert c_kv % (kv_rank + LANES) == 0 and c_qa % q_rank == 0
    wq = w_q_b.reshape(q_rank, heads_a, nope + rope)
    wq_rope = jnp.pad(wq[:, :, nope:], ((0, 0), (0, 0), (0, LANES - rope)))
    w_q = jnp.concatenate([wq[:, :, :nope].reshape(q_rank, heads_a * nope),
                           wq_rope.reshape(q_rank, heads_a * LANES)], axis=1)
    wuk_t = jnp.transpose(w_uk, (1, 2, 0))
    wuv_t = jnp.transpose(w_uv, (1, 0, 2))
    pos_all = jnp.concatenate([jnp.tile(jnp.arange(t, dtype=jnp.int32), b),
                               jnp.tile(past + jnp.arange(ts, dtype=jnp.int32), db)])
    cosf, sins = _rope_tables(pos_all, rope // 2)
    lam_in = jnp.zeros((8, LANES), F32)
    for i, v in enumerate((lambda_q1, lambda_k1, lambda_q2, lambda_k2)):
        lam_in = lam_in.at[i, :dh].set(v.astype(F32))
    slopes_b = _alibi_slopes(heads_b)
    slopes_c = _alibi_slopes(heads_c)
    t_col = jnp.asarray(np.repeat(np.arange(ts, dtype=np.float32), heads_a).reshape(ts * heads_a, 1))
    slope_col_b = jnp.asarray(np.tile(slopes_b, ts).reshape(ts * heads_b, 1))
    t_col_c = jnp.asarray(np.repeat(np.arange(ts, dtype=np.float32), rh).reshape(ts * rh, 1))
    slope_col_c = jnp.asarray(np.tile(slopes_c.reshape(groups, 1, rh), (1, ts, 1)).reshape(groups, ts * rh, 1))
    pt_flat = page_table.reshape(-1)
    cache_dk = cache_diff_k.reshape(n_pool, page, 2 * dh)
    cache_dv = cache_diff_v.reshape(n_pool, page, dv)
    cache_mk = cache_moba_k.reshape(n_pool, page * groups, d_c)
    cache_mv = cache_moba_v.reshape(n_pool, page * groups, d_c)
    cache_mla_t = jnp.swapaxes(cache_mla, 1, 2)

    n_c = b + db
    c_all = jnp.pad(jnp.concatenate([c_prompt, c_sample], axis=0), ((0, -n_c % 16), (0, 0)))
    mod = _ada(c_all, w_ada, b_ada).reshape(w_ada.shape[0], c_all.shape[0], 6, d_model)
    mods = [(mod[l, :b], mod[l, b:n_c]) for l in range(mod.shape[0])]

    x = jnp.concatenate([x_prompt.reshape(mp, d_model), x_sample.reshape(ms, d_model)], axis=0)
    x = x.reshape(m // ts, ts, d_model)
    pp = functools.partial(_postpre, dims=dims)

    def ffn_mats(h, l):
        a = _matmul(h.reshape(m, d_model), w_ff_up, BF16, bm=1024, bn=512, epilogue="relu2", layer=l)
        return _matmul(a, w_ff_down, F32, bm=1024, bn=1024, bk=2048, layer=l).reshape(x.shape)

    (h,) = pp(x, None, None, None, mods[0], g_pre_mix[0], gate_idx=2, pre_idx=0)
    z0 = _matmul(h.reshape(m, d_model), w0, F32)
    q = _matmul(z0, w_q, F32, a_cols=(c_qa, q_rank), gain=g_q_a)
    qscale_a = (nope + rope) ** -0.5 * LOG2E
    qscale_d = dh ** -0.5 * LOG2E
    q_lat, q_pe = _qhead(q, wuk_t, cosf, sins, heads=heads_a, qscale=qscale_a)
    rows_f, rows_bf = _kvrows(z0, c_kv // (kv_rank + LANES), g_kv_a, cosf, sins, rank=kv_rank)

    ol_p = _mla_prompt(q_lat, q_pe, rows_bf, b=b, t=t, heads=heads_a, rank=kv_rank)
    od_p = _diff_prompt(jnp.asarray(slopes_b), z0, lam_in, g_diff_sub, b=b, t=t, heads=heads_b,
                        k_blk=c_dk // LANES, v_blk=c_dv // LANES, dv=dv, qscale=qscale_d)
    r_ab = ts * heads_a
    ol_s, od_s = _ab_sample(
        pt_flat,
        q_lat[mp:].reshape(db, r_ab, kv_rank), q_pe[mp:].reshape(db, r_ab, LANES),
        z0[mp:, :c_qa].reshape(db, r_ab, 2 * dh),
        _pad_new(rows_bf[mp:], db, ts), _pad_new(z0[mp:, c_dk:c_dv], db, ts), _pad_new(z0[mp:, c_dv:], db, ts),
        slope_col_b, t_col, lam_in, g_diff_sub, cache_mla_t, cache_dk, cache_dv,
        n_pages=n_pages, rank=kv_rank, rope=rope, qscale_d=qscale_d)
    o_lat = jnp.concatenate([ol_p, ol_s.reshape(ms, heads_a * kv_rank)], axis=0)
    o_d = jnp.concatenate([od_p, od_s.reshape(ms, heads_b * dv)], axis=0)
    o_cat = jnp.concatenate([_headmm(o_lat, wuv_t, BF16), o_d], axis=1)
    mix = _matmul(o_cat, w_out_ab, F32, bm=1024, bn=512)
    x, h = pp(x, mix.reshape(x.shape), mods[0], g_post_mix[0], mods[0], g_pre_ffn[0], gate_idx=2, pre_idx=3)
    x, h = pp(x, ffn_mats(h, 0), mods[0], g_post_ffn[0], mods[1], g_pre_mix[1], gate_idx=5, pre_idx=0)

    zc = _matmul(h.reshape(m, d_model), w_in_c, F32)
    nq_c = heads_c * d_c
    qscale_c = d_c ** -0.5 * LOG2E
    oc_p = _moba_prompt(jnp.asarray(slopes_c), zc, b=b, t=t, heads=heads_c, groups=groups, d=d_c,
                        k_blk=heads_c, v_blk=heads_c + groups, qscale=qscale_c)
    q_s = zc[mp:, :nq_c].reshape(db, ts, groups, rh, d_c).transpose(0, 2, 1, 3, 4).reshape(db, groups, ts * rh, d_c)
    oc_s = _moba_sample(pt_flat, q_s, _pad_new(zc[mp:, nq_c:nq_c + groups * d_c], db, ts),
                        _pad_new(zc[mp:, nq_c + groups * d_c:], db, ts), slope_col_c, t_col_c,
                        cache_mk, cache_mv, n_pages=n_pages, groups=groups, d=d_c, qscale=qscale_c)
    oc_s = oc_s.reshape(db, groups, ts, rh, d_c).transpose(0, 2, 1, 3, 4).reshape(ms, nq_c)
    mix = _matmul(jnp.concatenate([oc_p, oc_s], axis=0), w_out_c, F32, bm=1024, bn=512)
    x, h = pp(x, mix.reshape(x.shape), mods[1], g_post_mix[1], mods[1], g_pre_ffn[1], gate_idx=2, pre_idx=3)
    y = ffn_mats(h, 1)
    (y_p,) = pp(x, y, mods[1], g_post_ffn[1], None, None, gate_idx=5, pre_idx=0, part="prompt")
    (y_s,) = pp(x, y, mods[1], g_post_ffn[1], None, None, gate_idx=5, pre_idx=0, part="sample")

    kd = groups * d_c
    return (y_p.reshape(b, t, d_model), y_s.reshape(db, ts, d_model),
            rows_f[:mp, :row_w].reshape(b, t, row_w),
            z0[:mp, c_dk:c_dv].reshape(b, t, 1, 2 * dh), z0[:mp, c_dv:].reshape(b, t, 1, dv),
            zc[:mp, nq_c:nq_c + kd].reshape(b, t, groups, d_c), zc[:mp, nq_c + kd:].reshape(b, t, groups, d_c),
            rows_f[mp:, :row_w].reshape(db, ts, row_w),
            z0[mp:, c_dk:c_dv].reshape(db, ts, 1, 2 * dh), z0[mp:, c_dv:].reshape(db, ts, 1, dv),
            zc[mp:, nq_c:nq_c + kd].reshape(db, ts, groups, d_c), zc[mp:, nq_c + kd:].reshape(db, ts, groups, d_c))
```

```python
import functools
import math

import numpy as np
import jax
import jax.numpy as jnp
from jax import lax
from jax.experimental import pallas as pl
from jax.experimental.pallas import tpu as pltpu

F32 = jnp.float32
BF16 = jnp.bfloat16
EPS = 1e-6
ROPE_THETA = 10000.0
MOBA_BLOCK = 256
BLOCK_SHIFT = MOBA_BLOCK.bit_length() - 1
MOBA_TOPK = 3
DIFF_LAMBDA_INIT = 0.8 - 0.6 * math.exp(-0.3 * 0)
LOG2E = math.log2(math.e)
NEG_INF = float("-inf")
MASKED = -1e30
LANES = 128
SUBLANES = 8
V7X_VMEM_BYTES = 64 * 1024 * 1024
VMEM_LIMIT = V7X_VMEM_BYTES - 8 * 1024 * 1024
NEW_KEY_PAD = 128
Q_TILE = 256
AB_PAGES_PER_STEP = 32
MOBA_PAGES_PER_STEP = 32

def _cp(*sem):
    return pltpu.CompilerParams(dimension_semantics=sem, vmem_limit_bytes=VMEM_LIMIT)


def _pick(n, pref):
    if n <= pref:
        return n
    d = pref - pref % LANES
    while d >= LANES:
        if n % d == 0:
            return d
        d -= LANES
    return n


def _dot(a, b):
    return jnp.dot(a, b, preferred_element_type=F32)


def _dot_nt(a, b):
    return lax.dot_general(a, b, (((1,), (1,)), ((), ())), preferred_element_type=F32)


def _dot3_nt(a, b):
    ah = a.astype(BF16)
    al = (a - ah.astype(F32)).astype(BF16)
    bh = b.astype(BF16)
    bl = (b - bh.astype(F32)).astype(BF16)
    return _dot_nt(ah, bh) + (_dot_nt(ah, bl) + _dot_nt(al, bh))


def _rms(x):
    return x * lax.rsqrt(jnp.mean(x * x, axis=-1, keepdims=True) + EPS)


def _ada_kernel(c_ref, w_ref, b_ref, o_ref):
    c = c_ref[...]
    a = (c * (1.0 / (1.0 + jnp.exp(-c)))).astype(BF16)
    o_ref[0] = _dot(a, w_ref[0].astype(BF16)) + b_ref[0]


def _ada(c_pad, w_ada, b_ada):
    nl, d, n = w_ada.shape
    bc = c_pad.shape[0]
    bn = _pick(n, 512)
    return pl.pallas_call(
        _ada_kernel,
        out_shape=jax.ShapeDtypeStruct((nl, bc, n), F32),
        grid=(nl, n // bn),
        in_specs=[pl.BlockSpec((bc, d), lambda l, j: (0, 0)),
                  pl.BlockSpec((1, d, bn), lambda l, j: (l, 0, j)),
                  pl.BlockSpec((1, 1, bn), lambda l, j: (l, 0, j))],
        out_specs=pl.BlockSpec((1, bc, bn), lambda l, j: (l, 0, j)),
        compiler_params=_cp("arbitrary", "arbitrary"),
        name="ada",
    )(c_pad, w_ada, b_ada.reshape(nl, 1, n))


def _postpre_kernel(*refs, first_sample_blk, has_y, has_h, gate_idx, pre_idx):
    it = iter(refs)
    x_ref = next(it)
    if has_y:
        y_ref, mpp_ref, mps_ref, gpost_ref = next(it), next(it), next(it), next(it)
    if has_h:
        mhp_ref, mhs_ref, gpre_ref = next(it), next(it), next(it)
    if has_y:
        xo_ref = next(it)
    if has_h:
        h_ref = next(it)

    def run(mpost_ref, mpre_ref):
        x = x_ref[...]
        if has_y:
            gate = mpost_ref[:, gate_idx:gate_idx + 1, :]
            x = x + gate * (_rms(y_ref[...]) * gpost_ref[...])
            xo_ref[...] = x
        if has_h:
            shift = mpre_ref[:, pre_idx:pre_idx + 1, :]
            scale = mpre_ref[:, pre_idx + 1:pre_idx + 2, :]
            h_ref[...] = ((_rms(x) * gpre_ref[...]) * (1.0 + scale) + shift).astype(h_ref.dtype)

    i = pl.program_id(0)

    @pl.when(i < first_sample_blk)
    def _():
        run(mpp_ref if has_y else None, mhp_ref if has_h else None)

    @pl.when(i >= first_sample_blk)
    def _():
        run(mps_ref if has_y else None, mhs_ref if has_h else None)


def _postpre(x, y, mod_post, g_post, mod_pre, g_pre, *, dims, gate_idx, pre_idx, part="all"):
    b, t, db, ts = dims
    ng, _, d = x.shape
    gb = max(1, 256 // ts)
    while db % gb or (t // ts) % gb:
        gb //= 2
    bpb = (t // ts) // gb
    nblk_p = b * bpb
    nblk = ng // gb
    lo, hi = {"all": (0, nblk), "prompt": (0, nblk_p), "sample": (nblk_p, nblk)}[part]
    has_y, has_h = y is not None, mod_pre is not None

    xspec = pl.BlockSpec((gb, ts, d), lambda i: (i + lo, 0, 0))
    ospec = pl.BlockSpec((gb, ts, d), lambda i: (i, 0, 0))
    pspec = pl.BlockSpec((1, 6, d), lambda i: (jnp.minimum((i + lo) // bpb, b - 1), 0, 0))
    sspec = pl.BlockSpec((gb, 6, d), lambda i: (jnp.maximum(i + lo - nblk_p, 0), 0, 0))
    gspec = pl.BlockSpec((1, 1, d), lambda i: (0, 0, 0))
    args, specs = [x], [xspec]
    if has_y:
        args += [y, mod_post[0], mod_post[1], g_post.reshape(1, 1, d)]
        specs += [xspec, pspec, sspec, gspec]
    if has_h:
        args += [mod_pre[0], mod_pre[1], g_pre.reshape(1, 1, d)]
        specs += [pspec, sspec, gspec]
    oshape = ((hi - lo) * gb, ts, d)
    out_shape, out_specs = [], []
    if has_y:
        out_shape.append(jax.ShapeDtypeStruct(oshape, F32))
        out_specs.append(ospec)
    if has_h:
        out_shape.append(jax.ShapeDtypeStruct(oshape, BF16))
        out_specs.append(ospec)
    outs = pl.pallas_call(
        functools.partial(_postpre_kernel, first_sample_blk=nblk_p - lo, has_y=has_y, has_h=has_h,
                          gate_idx=gate_idx, pre_idx=pre_idx),
        out_shape=out_shape, grid=(hi - lo,), in_specs=specs, out_specs=out_specs,
        compiler_params=_cp("arbitrary"), name="postpre",
    )(*args)
    return outs


def _mm_kernel(*refs, nk, prologue, epilogue):
    a_ref, w_ref = refs[0], refs[1]
    idx = 2
    if prologue == "rms":
        g_ref = refs[idx]
        idx += 1
    o_ref = refs[idx]
    acc_ref = refs[idx + 1] if nk > 1 else None

    a = a_ref[...]
    if prologue == "rms":
        a = _rms(a.astype(F32)) * g_ref[...]
    w = w_ref[...].reshape(w_ref.shape[-2:]).astype(BF16)

    def finish(v):
        if epilogue == "relu2":
            r = jnp.maximum(v, 0.0)
            v = r * r
        o_ref[...] = v.astype(o_ref.dtype)

    part = _dot(a.astype(BF16), w)
    if nk == 1:
        finish(part)
    else:
        k = pl.program_id(2)

        @pl.when(k == 0)
        def _():
            acc_ref[...] = part

        @pl.when(jnp.logical_and(k > 0, k < nk - 1))
        def _():
            acc_ref[...] += part

        @pl.when(k == nk - 1)
        def _():
            finish(acc_ref[...] + part)


def _matmul(a, w, out_dtype, *, bm=1024, bn=512, bk=None, a_cols=None, gain=None, epilogue=None, layer=None):
    m = a.shape[0]
    k, n = w.shape[-2:]
    bm = _pick(m, bm)
    bn = min(bn, n)
    bk = k if bk is None else _pick(k, bk)
    nk = k // bk
    col0 = 0
    if a_cols is not None:
        assert a_cols[1] == k and a_cols[0] % bk == 0
        col0 = a_cols[0] // bk
    prologue = None
    args = [a, w]
    if layer is None:
        wspec = pl.BlockSpec((bk, bn), lambda i, j, kk: (kk, j))
    else:
        wspec = pl.BlockSpec((1, bk, bn), lambda i, j, kk: (layer, kk, j))
    specs = [pl.BlockSpec((bm, bk), lambda i, j, kk: (i, col0 + kk)), wspec]
    if gain is not None:
        assert nk == 1
        prologue = "rms"
        args.append(gain.reshape(1, k))
        specs.append(pl.BlockSpec((1, k), lambda i, j, kk: (0, 0)))
    return pl.pallas_call(
        functools.partial(_mm_kernel, nk=nk, prologue=prologue, epilogue=epilogue),
        out_shape=jax.ShapeDtypeStruct((m, n), out_dtype),
        grid=(m // bm, pl.cdiv(n, bn), nk),
        in_specs=specs,
        out_specs=pl.BlockSpec((bm, bn), lambda i, j, kk: (i, j)),
        scratch_shapes=[pltpu.VMEM((bm, bn), F32)] if nk > 1 else [],
        compiler_params=_cp("arbitrary", "arbitrary", "arbitrary"),
        name="matmul",
    )(*args)


def _rope128(x, cosf, sins):
    lane = lax.broadcasted_iota(jnp.int32, x.shape, 1)
    first = (lane & 63) < 32
    swapped = jnp.where(first, pltpu.roll(x, 96, 1), pltpu.roll(x, 32, 1))
    return x * cosf + swapped * sins


def _qhead_kernel(qn_ref, qr_ref, wuk_ref, cos_ref, sin_ref, ql_ref, qp_ref, *, qscale):
    ql = _dot(qn_ref[...].astype(BF16), wuk_ref[0].astype(BF16))
    ql_ref[...] = (ql * qscale).astype(ql_ref.dtype)
    qp_ref[...] = (_rope128(qr_ref[...], cos_ref[...], sin_ref[...]) * qscale).astype(qp_ref.dtype)


def _qhead(q, wuk_t, cosf, sins, *, heads, qscale):
    m = q.shape[0]
    _, nope, rank = wuk_t.shape
    bm = _pick(m, 1024)
    return pl.pallas_call(
        functools.partial(_qhead_kernel, qscale=qscale),
        out_shape=[jax.ShapeDtypeStruct((m, heads * rank), BF16),
                   jax.ShapeDtypeStruct((m, heads * LANES), BF16)],
        grid=(m // bm, heads),
        in_specs=[pl.BlockSpec((bm, nope), lambda i, h: (i, h)),
                  pl.BlockSpec((bm, LANES), lambda i, h: (i, heads + h)),
                  pl.BlockSpec((1, nope, rank), lambda i, h: (h, 0, 0)),
                  pl.BlockSpec((bm, LANES), lambda i, h: (i, 0)),
                  pl.BlockSpec((bm, LANES), lambda i, h: (i, 0))],
        out_specs=[pl.BlockSpec((bm, rank), lambda i, h: (i, h)),
                   pl.BlockSpec((bm, LANES), lambda i, h: (i, h))],
        compiler_params=_cp("arbitrary", "arbitrary"),
        name="qhead",
    )(q, q, wuk_t, cosf, sins)


def _headmm_kernel(a_ref, w_ref, o_ref):
    o_ref[...] = _dot(a_ref[...].astype(BF16), w_ref[0].astype(BF16)).astype(o_ref.dtype)


def _headmm(a, w, out_dtype):
    m = a.shape[0]
    heads, k, n = w.shape
    bm = _pick(m, 1024)
    return pl.pallas_call(
        _headmm_kernel,
        out_shape=jax.ShapeDtypeStruct((m, heads * n), out_dtype),
        grid=(m // bm, heads),
        in_specs=[pl.BlockSpec((bm, k), lambda i, h: (i, h)),
                  pl.BlockSpec((1, k, n), lambda i, h: (h, 0, 0))],
        out_specs=pl.BlockSpec((bm, n), lambda i, h: (i, h)),
        compiler_params=_cp("arbitrary", "arbitrary"),
        name="headmm",
    )(a, w)


def _kvrows_kernel(z_ref, g_ref, cos_ref, sin_ref, of_ref, ob_ref, *, rank):
    z = z_ref[...]
    lat = _rms(z[:, :rank]) * g_ref[...]
    pe = _rope128(z[:, rank:], cos_ref[...], sin_ref[...])
    of_ref[:, :rank] = lat
    of_ref[:, rank:] = pe
    ob_ref[:, :rank] = lat.astype(BF16)
    ob_ref[:, rank:] = pe.astype(BF16)


def _kvrows(z, col_blk, g_kv, cosf, sins, *, rank):
    m = z.shape[0]
    w = rank + LANES
    bm = _pick(m, 1024)
    return pl.pallas_call(
        functools.partial(_kvrows_kernel, rank=rank),
        out_shape=[jax.ShapeDtypeStruct((m, w), F32), jax.ShapeDtypeStruct((m, w), BF16)],
        grid=(m // bm,),
        in_specs=[pl.BlockSpec((bm, w), lambda i: (i, col_blk)),
                  pl.BlockSpec((1, rank), lambda i: (0, 0)),
                  pl.BlockSpec((bm, LANES), lambda i: (i, 0)),
                  pl.BlockSpec((bm, LANES), lambda i: (i, 0))],
        out_specs=[pl.BlockSpec((bm, w), lambda i: (i, 0)), pl.BlockSpec((bm, w), lambda i: (i, 0))],
        compiler_params=_cp("arbitrary"),
        name="kvrows",
    )(z, g_kv.reshape(1, rank), cosf, sins)


def _flash_update(s, rc, v, m_ref, l_ref, acc_ref, rows=slice(None), v_transposed=False):
    m_old = m_ref[rows]
    m_new = jnp.maximum(m_old, jnp.max(s, axis=-1, keepdims=True) + rc)
    alpha = jnp.exp2(m_old - m_new)
    p = jnp.exp2(s - (m_new - rc))
    l_ref[rows] = alpha * l_ref[rows] + jnp.sum(p, axis=-1, keepdims=True)
    pb = p.astype(BF16)
    acc_ref[rows] = alpha * acc_ref[rows] + (_dot_nt(pb, v) if v_transposed else _dot(pb, v))
    m_ref[rows] = m_new


def _flash_init(m_ref, l_ref, acc_ref):
    m_ref[...] = jnp.full(m_ref.shape, NEG_INF, F32)
    l_ref[...] = jnp.zeros(l_ref.shape, F32)
    acc_ref[...] = jnp.zeros(acc_ref.shape, F32)


def _lambda_value(lam_ref):
    lv = lam_ref[...]
    d1 = jnp.sum(lv[0:1] * lv[1:2], axis=-1, keepdims=True)
    d2 = jnp.sum(lv[2:3] * lv[3:4], axis=-1, keepdims=True)
    return jnp.exp(d1) - jnp.exp(d2) + DIFF_LAMBDA_INIT


def _split_halves(q):
    lane = lax.broadcasted_iota(jnp.int32, q.shape, 1)
    lo = lane < (q.shape[1] // 2)
    zero = jnp.zeros_like(q)
    return jnp.where(lo, q, zero), jnp.where(lo, zero, q)


def _diff_finish(acc1, l1, acc2, l2, lam, g):
    a = acc1 / l1 - lam * (acc2 / l2)
    return (_rms(a) * g) * (1.0 - DIFF_LAMBDA_INIT)


def _flash_update_t(st, rc, vt, m_ref, l_ref, acc_ref, cols=slice(None)):
    m_old = m_ref[:, cols]
    m_new = jnp.maximum(m_old, jnp.max(st, axis=0, keepdims=True) + rc)
    alpha = jnp.exp2(m_old - m_new)
    p = jnp.exp2(st - (m_new - rc))
    l_ref[:, cols] = alpha * l_ref[:, cols] + jnp.sum(p, axis=0, keepdims=True)
    acc_ref[:, cols] = alpha * acc_ref[:, cols] + _dot(vt, p.astype(BF16))
    m_ref[:, cols] = m_new


def _causal_keep_t(tq, cols):
    key = lax.broadcasted_iota(jnp.int32, (tq, cols), 0)
    pos = lax.broadcasted_iota(jnp.int32, (tq, cols), 1) & (tq - 1)
    return key <= pos


def _rel_bias_t(tq):
    key = lax.broadcasted_iota(jnp.int32, (tq, tq), 0)
    pos = lax.broadcasted_iota(jnp.int32, (tq, tq), 1)
    return (key - pos).astype(F32)


def _transpose_bf16(x):
    return x.astype(F32).T.astype(BF16)


def _mla_prompt_kernel(ql_ref, qp_ref, k_ref, o_ref, qs, kt, m_ref, l_ref, acc_ref, *, tq, nq, heads, rank):
    qi = pl.program_id(1)

    @pl.when(qi == 0)
    def _():
        for c in range(nq):
            kt[c] = _transpose_bf16(k_ref[c * tq:(c + 1) * tq, :rank])

    for h in range(heads):
        rows = slice(h * tq, (h + 1) * tq)
        qs[rows, :rank] = ql_ref[:, h * rank:(h + 1) * rank]
        qs[rows, rank:] = qp_ref[:, h * LANES:(h + 1) * LANES]
    _flash_init(m_ref, l_ref, acc_ref)

    def body(c, carry):
        kc = k_ref[pl.ds(pl.multiple_of(c * tq, tq), tq), :]
        _flash_update_t(_dot_nt(kc, qs[...]), 0.0, kt[c], m_ref, l_ref, acc_ref)
        return carry

    lax.fori_loop(0, qi, body, 0)
    kc = k_ref[pl.ds(pl.multiple_of(qi * tq, tq), tq), :]
    st = jnp.where(_causal_keep_t(tq, heads * tq), _dot_nt(kc, qs[...]), NEG_INF)
    _flash_update_t(st, 0.0, kt[qi], m_ref, l_ref, acc_ref)
    for h in range(heads):
        cols = slice(h * tq, (h + 1) * tq)
        o_ref[:, h * rank:(h + 1) * rank] = (acc_ref[:, cols] / l_ref[:, cols]).T.astype(o_ref.dtype)


def _mla_prompt(q_lat, q_pe, rows_bf, *, b, t, heads, rank):
    tq = Q_TILE
    assert t % tq == 0
    nq = t // tq
    w = rank + LANES
    r = heads * tq
    return pl.pallas_call(
        functools.partial(_mla_prompt_kernel, tq=tq, nq=nq, heads=heads, rank=rank),
        out_shape=jax.ShapeDtypeStruct((b * t, heads * rank), BF16),
        grid=(b, nq),
        in_specs=[pl.BlockSpec((tq, heads * rank), lambda bb, i: (bb * nq + i, 0)),
                  pl.BlockSpec((tq, heads * LANES), lambda bb, i: (bb * nq + i, 0)),
                  pl.BlockSpec((t, w), lambda bb, i: (bb, 0))],
        out_specs=pl.BlockSpec((tq, heads * rank), lambda bb, i: (bb * nq + i, 0)),
        scratch_shapes=[pltpu.VMEM((r, w), BF16), pltpu.VMEM((nq, rank, tq), BF16),
                        pltpu.VMEM((1, r), F32), pltpu.VMEM((1, r), F32), pltpu.VMEM((rank, r), F32)],
        compiler_params=_cp("arbitrary", "arbitrary"),
        name="mla_prompt",
    )(q_lat, q_pe, rows_bf)


def _diff_prompt_kernel(slopes_ref, q_ref, k_ref, v_ref, lam_ref, g_ref, o_ref,
                        qs, kb, vt, relb, slr, m_ref, l_ref, acc_ref, *, tq, nq, heads, dv, qscale):
    qi = pl.program_id(1)
    hr = heads * tq

    @pl.when(qi == 0)
    def _():
        kb[...] = k_ref[...].astype(BF16)
        for c in range(nq):
            vt[c] = v_ref[c * tq:(c + 1) * tq, :].T.astype(BF16)
        rel = _rel_bias_t(tq)
        for h in range(heads):
            s2 = slopes_ref[h] * LOG2E
            relb[:, h * tq:(h + 1) * tq] = s2 * rel
            slr[:, h * tq:(h + 1) * tq] = jnp.full((1, tq), s2, F32)

    for h in range(heads):
        q1, q2 = _split_halves((q_ref[:, h * LANES:(h + 1) * LANES] * qscale).astype(BF16))
        qs[h * tq:(h + 1) * tq, :] = q1
        qs[hr + h * tq:hr + (h + 1) * tq, :] = q2
    _flash_init(m_ref, l_ref, acc_ref)
    halves = (slice(0, hr), slice(hr, 2 * hr))

    def body(c, carry):
        k = kb[pl.ds(pl.multiple_of(c * tq, tq), tq), :]
        rc = slr[...] * ((c - qi) * tq).astype(F32)
        for half in halves:
            _flash_update_t(_dot_nt(k, qs[half]) + relb[...], rc, vt[c], m_ref, l_ref, acc_ref, half)
        return carry

    lax.fori_loop(0, qi, body, 0)
    k = kb[pl.ds(pl.multiple_of(qi * tq, tq), tq), :]
    keep = _causal_keep_t(tq, hr)
    for half in halves:
        st = jnp.where(keep, _dot_nt(k, qs[half]) + relb[...], NEG_INF)
        _flash_update_t(st, 0.0, vt[qi], m_ref, l_ref, acc_ref, half)

    lam = _lambda_value(lam_ref)
    for h in range(heads):
        c1 = slice(h * tq, (h + 1) * tq)
        c2 = slice(hr + h * tq, hr + (h + 1) * tq)
        a = acc_ref[:, c1] / l_ref[:, c1] - lam * (acc_ref[:, c2] / l_ref[:, c2])
        o_ref[:, h * dv:(h + 1) * dv] = (
            (_rms(a.T) * g_ref[...]) * (1.0 - DIFF_LAMBDA_INIT)).astype(o_ref.dtype)


def _diff_prompt(slopes, z0, lam_in, g_sub, *, b, t, heads, k_blk, v_blk, dv, qscale):
    tq = Q_TILE
    assert t % tq == 0
    nq = t // tq
    hr = heads * tq
    return pl.pallas_call(
        functools.partial(_diff_prompt_kernel, tq=tq, nq=nq, heads=heads, dv=dv, qscale=qscale),
        out_shape=jax.ShapeDtypeStruct((b * t, heads * dv), BF16),
        grid=(b, nq),
        in_specs=[pl.BlockSpec(memory_space=pltpu.SMEM),
                  pl.BlockSpec((tq, heads * LANES), lambda bb, i: (bb * nq + i, 0)),
                  pl.BlockSpec((t, LANES), lambda bb, i: (bb, k_blk)),
                  pl.BlockSpec((t, dv), lambda bb, i: (bb, v_blk)),
                  pl.BlockSpec((8, LANES), lambda bb, i: (0, 0)),
                  pl.BlockSpec((1, dv), lambda bb, i: (0, 0))],
        out_specs=pl.BlockSpec((tq, heads * dv), lambda bb, i: (bb * nq + i, 0)),
        scratch_shapes=[pltpu.VMEM((2 * hr, LANES), BF16), pltpu.VMEM((t, LANES), BF16),
                        pltpu.VMEM((nq, dv, tq), BF16), pltpu.VMEM((tq, hr), F32), pltpu.VMEM((1, hr), F32),
                        pltpu.VMEM((1, 2 * hr), F32), pltpu.VMEM((1, 2 * hr), F32),
                        pltpu.VMEM((dv, 2 * hr), F32)],
        compiler_params=_cp("arbitrary", "arbitrary"),
        name="diff_prompt",
    )(slopes, z0, z0, z0, lam_in, g_sub.reshape(1, dv))


def _ab_sample_kernel(pt_ref, ql_ref, qp_ref, qd_ref, kn_ref, dkn_ref, dvn_ref, slope_ref, t_ref,
                      tab_ref, lam_ref, g_ref, cm_hbm, ck_hbm, cv_hbm, ol_ref, od_ref,
                      pg_m, pg_k, pg_v, sem, kt, dkb, dvb, m_a, l_a, acc_a, m_d, l_d, acc_d,
                      *, npg, n_pages, n_seq, page, rank, rope, past, nchunk, qscale_d):
    b = pl.program_id(0)
    j = pl.program_id(1)
    step = b * nchunk + j
    slot = step & 1
    kc = npg * page
    row_w = rank + rope

    def page_copies(page_id, slt, p):
        return (pltpu.make_async_copy(cm_hbm.at[page_id], pg_m.at[slt, p], sem.at[slt]),
                pltpu.make_async_copy(ck_hbm.at[page_id], pg_k.at[slt, p], sem.at[slt]),
                pltpu.make_async_copy(cv_hbm.at[page_id], pg_v.at[slt, p], sem.at[slt]))

    def start_gather(bb, jj, slt):
        for p in range(npg):
            for cp in page_copies(pt_ref[bb * n_pages + jj * npg + p], slt, p):
                cp.start()

    @pl.when(step == 0)
    def _():
        start_gather(b, j, slot)

    @pl.when(step + 1 < n_seq * nchunk)
    def _():
        last = j == nchunk - 1
        start_gather(jnp.where(last, b + 1, b), jnp.where(last, 0, j + 1), 1 - slot)

    for p in range(npg):
        for cp in page_copies(0, slot, p):
            cp.wait()

    @pl.when(j == 0)
    def _():
        _flash_init(m_a, l_a, acc_a)
        _flash_init(m_d, l_d, acc_d)
        kt[row_w:, :] = jnp.zeros((kt.shape[0] - row_w, kc), BF16)

    for p in range(npg):
        cols = slice(p * page, (p + 1) * page)
        kt[:row_w, cols] = pg_m[slot, p].astype(BF16)
        dkb[cols, :] = pg_k[slot, p].astype(BF16)
        dvb[cols, :] = pg_v[slot, p].astype(BF16)

    ql = ql_ref[0]
    qp = qp_ref[0]
    q1, q2 = _split_halves((qd_ref[0] * qscale_d).astype(BF16))
    qd = jnp.concatenate([q1, q2], axis=0)
    slope2 = slope_ref[...] * LOG2E
    tq = t_ref[...]
    r = ql.shape[0]

    rc = slope2 * ((j * kc - past).astype(F32) - tq)
    lat_t = kt[:rank, :]
    _flash_update(_dot(ql, lat_t) + _dot(qp, kt[rank:, :]), 0.0, lat_t, m_a, l_a, acc_a, v_transposed=True)
    _flash_update(_dot_nt(qd, dkb[...]) + tab_ref[...], rc, dvb[...], m_d, l_d, acc_d)

    @pl.when(j == nchunk - 1)
    def _():
        kn = kn_ref[0]
        nk = kn.shape[0]
        tk = lax.broadcasted_iota(jnp.int32, (2 * r, nk), 1).astype(F32)
        keep = tk <= tq
        latn = kn[:, :rank]
        sn = _dot_nt(ql, latn) + _dot_nt(qp, kn[:, rank:])
        _flash_update(jnp.where(keep[:r], sn, NEG_INF), 0.0, latn, m_a, l_a, acc_a)
        sd = _dot_nt(qd, dkn_ref[0]) + slope2 * (tk - tq)
        _flash_update(jnp.where(keep, sd, NEG_INF), 0.0, dvn_ref[0], m_d, l_d, acc_d)
        ol_ref[0] = (acc_a[...] / l_a[...]).astype(ol_ref.dtype)
        od_ref[0] = _diff_finish(acc_d[:r], l_d[:r], acc_d[r:], l_d[r:],
                                 _lambda_value(lam_ref), g_ref[...]).astype(od_ref.dtype)


def _pages_per_step(n_pages, most):
    for p in (32, 16, 8, 4, 2, 1):
        if p > most:
            continue
        if n_pages % p == 0 and n_pages // p >= 2:
            return p
    return 1


def _ab_sample(pt_flat, ql, qp, qd, kn, dkn, dvn, slope_col, t_col, lam_in, g_sub,
               cache_mla, cache_dk, cache_dv, *, n_pages, rank, rope, qscale_d):
    db, r, _ = ql.shape
    row_w, page = cache_mla.shape[1:]
    dv = cache_dv.shape[2]
    npg = _pages_per_step(n_pages, AB_PAGES_PER_STEP)
    nchunk = n_pages // npg
    kc = npg * page
    past = n_pages * page
    slope2 = jnp.concatenate([slope_col, slope_col], axis=0)
    t2 = jnp.concatenate([t_col, t_col], axis=0)
    tab = (slope2 * LOG2E) * jnp.arange(kc, dtype=F32)[None, :]

    def seq_spec(shape):
        return pl.BlockSpec((1,) + shape, lambda b, j, pt: (b, 0, 0))

    def const_spec(shape):
        return pl.BlockSpec(shape, lambda b, j, pt: (0, 0))

    in_specs = [seq_spec((r, rank)), seq_spec((r, LANES)), seq_spec((r, LANES)),
                seq_spec((NEW_KEY_PAD, rank + LANES)), seq_spec((NEW_KEY_PAD, LANES)),
                seq_spec((NEW_KEY_PAD, dv)),
                const_spec((2 * r, 1)), const_spec((2 * r, 1)),
                pl.BlockSpec((2 * r, kc), lambda b, j, pt: (0, 0), pipeline_mode=pl.Buffered(1)),
                const_spec((8, LANES)), const_spec((1, dv))]
    in_specs += [pl.BlockSpec(memory_space=pl.ANY)] * 3
    scratch = [pltpu.VMEM((2, npg, row_w, page), F32), pltpu.VMEM((2, npg, page, LANES), F32),
               pltpu.VMEM((2, npg, page, dv), F32), pltpu.SemaphoreType.DMA((2,)),
               pltpu.VMEM((rank + LANES, kc), BF16),
               pltpu.VMEM((kc, LANES), BF16), pltpu.VMEM((kc, dv), BF16),
               pltpu.VMEM((r, 1), F32), pltpu.VMEM((r, 1), F32), pltpu.VMEM((r, rank), F32),
               pltpu.VMEM((2 * r, 1), F32), pltpu.VMEM((2 * r, 1), F32), pltpu.VMEM((2 * r, dv), F32)]
    return pl.pallas_call(
        functools.partial(_ab_sample_kernel, npg=npg, n_pages=n_pages, n_seq=db, page=page, rank=rank,
                          rope=rope, past=past, nchunk=nchunk, qscale_d=qscale_d),
        out_shape=[jax.ShapeDtypeStruct((db, r, rank), BF16), jax.ShapeDtypeStruct((db, r, dv), BF16)],
        grid_spec=pltpu.PrefetchScalarGridSpec(
            num_scalar_prefetch=1, grid=(db, nchunk), in_specs=in_specs,
            out_specs=[seq_spec((r, rank)), seq_spec((r, dv))], scratch_shapes=scratch),
        compiler_params=_cp("arbitrary", "arbitrary"),
        name="ab_sample",
    )(pt_flat, ql, qp, qd, kn, dkn, dvn, slope2, t2, tab, lam_in, g_sub.reshape(1, dv),
      cache_mla, cache_dk, cache_dv)


def _topk_mask_t(gate_ref, nblk, n_valid, n_sel, always):
    gate = gate_ref[...]
    blk = lax.broadcasted_iota(jnp.int32, gate.shape, 0)

    def body(k, rank):
        gk = gate_ref[pl.ds(k, 1), :]
        beats = jnp.logical_or(gk > gate, jnp.logical_and(gk == gate, k < blk))
        return rank + jnp.where(jnp.logical_and(beats, k < n_valid), 1.0, 0.0)

    rank = lax.fori_loop(0, nblk, body, jnp.zeros(gate.shape, F32), unroll=8 if nblk % 8 == 0 else 1)
    chosen = jnp.logical_and(blk < n_valid, rank < float(n_sel))
    return jnp.where(jnp.logical_or(chosen, blk == always), 0.0, MASKED)


def _moba_prompt_kernel(slopes_ref, q_ref, k_ref, v_ref, o_ref,
                        qs, kaug, vt, kmean, relb, slr, gsc, m_ref, l_ref, acc_ref,
                        *, blk, nblk, rh, d, qscale):
    g = pl.program_id(1)
    qi = pl.program_id(2)
    hr = rh * blk

    @pl.when(qi == 0)
    def _():
        t = k_ref.shape[0]
        kaug[:, :d] = k_ref[...].astype(BF16)
        krow_blk = lax.broadcasted_iota(jnp.int32, (t, d), 0) >> BLOCK_SHIFT
        lane = lax.broadcasted_iota(jnp.int32, (t, d), 1)
        kaug[:, d:] = jnp.where((lane & 7) == krow_blk, 1.0, 0.0).astype(BF16)
        kmean[...] = jnp.zeros(kmean.shape, F32)
        for n in range(nblk):
            vt[n] = v_ref[n * blk:(n + 1) * blk, :].T.astype(BF16)
            kmean[n:n + 1, :] = jnp.mean(k_ref[n * blk:(n + 1) * blk, :], axis=0, keepdims=True)
        rel = _rel_bias_t(blk)
        for h in range(rh):
            s2 = slopes_ref[g * rh + h] * LOG2E
            relb[:, h * blk:(h + 1) * blk] = s2 * rel
            slr[:, h * blk:(h + 1) * blk] = jnp.full((1, blk), s2, F32)

    km = kmean[...]
    for h in range(rh):
        qh = q_ref[:, h * d:(h + 1) * d]
        qs[h * blk:(h + 1) * blk, :d] = (qh * qscale).astype(BF16)
        gsc[h * SUBLANES:(h + 1) * SUBLANES, :] = _dot3_nt(km, qh)

    gate = gsc[...].reshape(rh, SUBLANES, blk)
    bidx = lax.broadcasted_iota(jnp.int32, gate.shape, 1)
    rank = jnp.zeros(gate.shape, F32)
    for k in range(nblk):
        gk = gate[:, k:k + 1, :]
        beats = jnp.logical_or(gk > gate, jnp.logical_and(gk == gate, k < bidx))
        rank = rank + jnp.where(jnp.logical_and(beats, k < qi), 1.0, 0.0)
    chosen = jnp.logical_and(bidx < qi, rank < float(min(MOBA_TOPK, nblk - 1)))
    sel = jnp.where(jnp.logical_or(chosen, bidx == qi), 0.0, MASKED)
    sel_rows = sel.reshape(rh * SUBLANES, blk).T.astype(BF16)
    lane_head = lax.broadcasted_iota(jnp.int32, sel_rows.shape, 1) >> 3
    zero = jnp.zeros_like(sel_rows)
    for h in range(rh):
        qs[h * blk:(h + 1) * blk, d:] = jnp.where(lane_head == h, sel_rows, zero)

    _flash_init(m_ref, l_ref, acc_ref)
    own = pl.multiple_of(qi * blk, blk)
    st = jnp.where(_causal_keep_t(blk, hr), _dot_nt(kaug[pl.ds(own, blk), :], qs[...]) + relb[...], NEG_INF)
    _flash_update_t(st, 0.0, vt[qi], m_ref, l_ref, acc_ref)

    def body(jb, carry):
        off = pl.multiple_of(jb * blk, blk)
        rc = slr[...] * ((jb - qi) * blk).astype(F32)
        _flash_update_t(_dot_nt(kaug[pl.ds(off, blk), :], qs[...]) + relb[...], rc, vt[jb],
                        m_ref, l_ref, acc_ref)
        return carry

    lax.fori_loop(0, qi, body, 0)
    for h in range(rh):
        cols = slice(h * blk, (h + 1) * blk)
        o_ref[:, h * d:(h + 1) * d] = (acc_ref[:, cols] / l_ref[:, cols]).T.astype(o_ref.dtype)


def _moba_prompt(slopes, zc, *, b, t, heads, groups, d, k_blk, v_blk, qscale):
    blk = MOBA_BLOCK
    assert t % blk == 0
    nblk = t // blk
    rh = heads // groups
    assert nblk <= SUBLANES and rh * SUBLANES == LANES and d == LANES
    hr = rh * blk
    return pl.pallas_call(
        functools.partial(_moba_prompt_kernel, blk=blk, nblk=nblk, rh=rh, d=d, qscale=qscale),
        out_shape=jax.ShapeDtypeStruct((b * t, heads * d), BF16),
        grid=(b, groups, nblk),
        in_specs=[pl.BlockSpec(memory_space=pltpu.SMEM),
                  pl.BlockSpec((blk, rh * d), lambda bb, g, i: (bb * nblk + i, g)),
                  pl.BlockSpec((t, d), lambda bb, g, i: (bb, k_blk + g)),
                  pl.BlockSpec((t, d), lambda bb, g, i: (bb, v_blk + g))],
        out_specs=pl.BlockSpec((blk, rh * d), lambda bb, g, i: (bb * nblk + i, g)),
        scratch_shapes=[pltpu.VMEM((hr, 2 * d), BF16), pltpu.VMEM((t, 2 * d), BF16),
                        pltpu.VMEM((nblk, d, blk), BF16),
                        pltpu.VMEM((SUBLANES, d), F32), pltpu.VMEM((blk, hr), F32), pltpu.VMEM((1, hr), F32),
                        pltpu.VMEM((rh * SUBLANES, blk), F32),
                        pltpu.VMEM((1, hr), F32), pltpu.VMEM((1, hr), F32), pltpu.VMEM((d, hr), F32)],
        compiler_params=_cp("arbitrary", "arbitrary", "arbitrary"),
        name="moba_prompt",
    )(slopes, zc, zc, zc)


def _moba_sample_kernel(pt_ref, q_ref, kn_ref, vn_ref, slope_ref, t_ref, tab_ref, ck_hbm, cv_hbm, o_ref,
                        pages, sem, s_all, kb, vb, ksum, gsc, selb, m_s, l_s, acc_s,
                        *, npg, n_pages, n_seq, page, groups, d, past, nchunk, nfull, qscale):
    b = pl.program_id(0)
    j = pl.program_id(1)
    nsteps = 2 * nchunk
    step = b * nsteps + j
    slot = step & 1
    kc = npg * page
    ppb = MOBA_BLOCK // page
    bpc = kc // MOBA_BLOCK
    r = q_ref.shape[2]
    tq = t_ref[...]

    def page_copy(cache, page_id, slt, p):
        return pltpu.make_async_copy(cache.at[page_id], pages.at[slt, p], sem.at[slt])

    def start_gather(bb, jj, slt):
        @pl.when(jj < nchunk)
        def _():
            for p in range(npg):
                page_copy(ck_hbm, pt_ref[bb * n_pages + jj * npg + p], slt, p).start()

        @pl.when(jj >= nchunk)
        def _():
            for p in range(npg):
                page_copy(cv_hbm, pt_ref[bb * n_pages + (jj - nchunk) * npg + p], slt, p).start()

    @pl.when(step == 0)
    def _():
        start_gather(b, j, slot)

    @pl.when(step + 1 < n_seq * nsteps)
    def _():
        last = j == nsteps - 1
        start_gather(jnp.where(last, b + 1, b), jnp.where(last, 0, j + 1), 1 - slot)

    for p in range(npg):
        page_copy(ck_hbm, 0, slot, p).wait()

    @pl.when(j == 0)
    def _():
        ksum[...] = jnp.zeros(ksum.shape, F32)

    @pl.when(j < nchunk)
    def _():
        for g in range(groups):
            csum = None
            for p in range(npg):
                x = pages[slot, p, pl.ds(g, page, stride=groups), :]
                kb[p * page:(p + 1) * page, g * d:(g + 1) * d] = x.astype(BF16)
                cs = jnp.sum(x, axis=0, keepdims=True)
                csum = cs if p % ppb == 0 else csum + cs
                if p % ppb == ppb - 1:
                    ksum[g, pl.ds(j * bpc + p // ppb, 1), :] = csum
        for g in range(groups):
            qb = (q_ref[0, g] * qscale).astype(BF16)
            s_all[g, j] = _dot_nt(qb, kb[:, g * d:(g + 1) * d]) + tab_ref[g]

    @pl.when(j == nchunk - 1)
    def _():
        kn = kn_ref[0]
        vn = vn_ref[0]
        nk = kn.shape[0]
        tk = lax.broadcasted_iota(jnp.int32, (r, nk), 1).astype(F32)
        keep = tk <= tq
        for g in range(groups):
            q = q_ref[0, g]
            nb = gsc.shape[0]
            gsc[...] = _dot3_nt(ksum[g, :nb] * (1.0 / MOBA_BLOCK), q)
            mask = _topk_mask_t(gsc, nfull, nfull, min(MOBA_TOPK, nfull), -1)
            if nb < LANES:
                mask = jnp.concatenate([mask, jnp.zeros((LANES - nb, mask.shape[1]), F32)], axis=0)
            selb[g] = mask.T.astype(BF16)
            m_ref, l_ref, acc_ref = m_s.at[g], l_s.at[g], acc_s.at[g]
            _flash_init(m_ref, l_ref, acc_ref)
            sn = _dot_nt((q * qscale).astype(BF16), kn[:, g * d:(g + 1) * d]) + (slope_ref[g] * LOG2E) * (tk - tq)
            _flash_update(jnp.where(keep, sn, NEG_INF), 0.0, vn[:, g * d:(g + 1) * d], m_ref, l_ref, acc_ref)

    @pl.when(j >= nchunk)
    def _():
        c = j - nchunk
        for g in range(groups):
            for p in range(npg):
                vb[p * page:(p + 1) * page, g * d:(g + 1) * d] = (
                    pages[slot, p, pl.ds(g, page, stride=groups), :].astype(BF16))
        kblk = c * bpc + (lax.broadcasted_iota(jnp.int32, (LANES, kc), 1) >> BLOCK_SHIFT)
        ind = jnp.where(lax.broadcasted_iota(jnp.int32, (LANES, kc), 0) == kblk, 1.0, 0.0).astype(BF16)
        for g in range(groups):
            rc = (slope_ref[g] * LOG2E) * ((c * kc - past).astype(F32) - tq)
            _flash_update(s_all[g, c] + _dot(selb[g], ind), rc, vb[:, g * d:(g + 1) * d],
                          m_s.at[g], l_s.at[g], acc_s.at[g])

    @pl.when(j == 2 * nchunk - 1)
    def _():
        for g in range(groups):
            o_ref[0, g] = (acc_s[g] / l_s[g]).astype(o_ref.dtype)


def _moba_sample(pt_flat, q, kn, vn, slope_col, t_col, cache_k, cache_v, *, n_pages, groups, d, qscale):
    db, _, r, _ = q.shape
    page = cache_k.shape[1] // groups
    assert MOBA_BLOCK % page == 0
    past = n_pages * page
    assert past % MOBA_BLOCK == 0
    nfull = past // MOBA_BLOCK
    assert 1 <= nfull <= LANES
    npg = _pages_per_step(n_pages, MOBA_PAGES_PER_STEP)
    while (npg * page) % MOBA_BLOCK:
        npg *= 2
    nchunk = n_pages // npg
    kc = npg * page
    gd = groups * d
    tab = (slope_col * LOG2E) * jnp.arange(kc, dtype=F32)[None, None, :]

    in_specs = [pl.BlockSpec((1, groups, r, d), lambda b, j, pt: (b, 0, 0, 0)),
                pl.BlockSpec((1, NEW_KEY_PAD, gd), lambda b, j, pt: (b, 0, 0)),
                pl.BlockSpec((1, NEW_KEY_PAD, gd), lambda b, j, pt: (b, 0, 0)),
                pl.BlockSpec((groups, r, 1), lambda b, j, pt: (0, 0, 0)),
                pl.BlockSpec((r, 1), lambda b, j, pt: (0, 0)),
                pl.BlockSpec((groups, r, kc), lambda b, j, pt: (0, 0, 0), pipeline_mode=pl.Buffered(1)),
                pl.BlockSpec(memory_space=pl.ANY), pl.BlockSpec(memory_space=pl.ANY)]
    scratch = [pltpu.VMEM((2, npg, page * groups, d), F32), pltpu.SemaphoreType.DMA((2,)),
               pltpu.VMEM((groups, nchunk, r, kc), F32),
               pltpu.VMEM((kc, gd), BF16), pltpu.VMEM((kc, gd), BF16),
               pltpu.VMEM((groups, LANES, d), F32), pltpu.VMEM((-(-nfull // SUBLANES) * SUBLANES, r), F32),
               pltpu.VMEM((groups, r, LANES), BF16),
               pltpu.VMEM((groups, r, 1), F32), pltpu.VMEM((groups, r, 1), F32),
               pltpu.VMEM((groups, r, d), F32)]
    return pl.pallas_call(
        functools.partial(_moba_sample_kernel, npg=npg, n_pages=n_pages, n_seq=db, page=page, groups=groups,
                          d=d, past=past, nchunk=nchunk, nfull=nfull, qscale=qscale),
        out_shape=jax.ShapeDtypeStruct((db, groups, r, d), BF16),
        grid_spec=pltpu.PrefetchScalarGridSpec(
            num_scalar_prefetch=1, grid=(db, 2 * nchunk), in_specs=in_specs,
            out_specs=pl.BlockSpec((1, groups, r, d), lambda b, j, pt: (b, 0, 0, 0)),
            scratch_shapes=scratch),
        compiler_params=_cp("arbitrary", "arbitrary"),
        name="moba_sample",
    )(pt_flat, q, kn, vn, slope_col, t_col, tab, cache_k, cache_v)


def _alibi_slopes(n):
    return np.asarray(2.0 ** (-8.0 * np.arange(1, n + 1) / n), dtype=np.float32)


def _rope_tables(pos, half):
    inv = ROPE_THETA ** (-jnp.arange(half, dtype=F32) / half)
    ang = pos.astype(F32)[:, None] * inv
    cos, sin = jnp.cos(ang), jnp.sin(ang)
    reps = LANES // (2 * half)
    return jnp.tile(cos, (1, 2 * reps)), jnp.tile(jnp.concatenate([-sin, sin], axis=-1), (1, reps))


def _pad_new(x, db, ts):
    x = x.reshape(db, ts, x.shape[-1]).astype(BF16)
    return jnp.pad(x, ((0, 0), (0, NEW_KEY_PAD - ts), (0, 0)))


def kernel(x_prompt, x_sample, cache_mla, cache_diff_k, cache_diff_v, cache_moba_k, cache_moba_v,
           page_table, c_prompt, c_sample, w_ada, b_ada, g_pre_mix, g_post_mix, g_pre_ffn, g_post_ffn,
           w_in_ab, g_q_a, w_q_b, g_kv_a, w_uk, w_uv, lambda_q1, lambda_k1, lambda_q2, lambda_k2,
           g_diff_sub, w_out_ab, w_in_c, w_out_c, w_ff_up, w_ff_down):
    b, t, d_model = x_prompt.shape
    db, ts, _ = x_sample.shape
    mp, ms = b * t, db * ts
    m = mp + ms
    dims = (b, t, db, ts)
    n_pool, page, row_w = cache_mla.shape
    n_pages = page_table.shape[1]
    past = n_pages * page
    q_rank = g_q_a.shape[0]
    kv_rank, heads_a, nope = w_uk.shape
    rope = row_w - kv_rank
    dh = lambda_q1.shape[0]
    dv = g_diff_sub.shape[0]
    heads_b = (w_in_ab.shape[1] - q_rank - row_w - 2 * dh - dv) // (2 * dh)
    groups, d_c = cache_moba_k.shape[2], cache_moba_k.shape[3]
    heads_c = w_out_c.shape[0] // d_c
    rh = heads_c // groups
    assert nope == LANES and 2 * rope == LANES and 2 * dh == LANES and dv == LANES and d_c == LANES
    assert kv_rank % LANES == 0 and heads_a == heads_b and cache_diff_k.shape[2] == 1
    assert ts <= NEW_KEY_PAD

    o1, o2 = q_rank, q_rank + row_w
    o3 = o2 + heads_b * 2 * dh
    w_kv = jnp.pad(w_in_ab[:, o1:o2], ((0, 0), (0, LANES - rope)))
    w0 = jnp.concatenate([w_in_ab[:, o2:o3], w_in_ab[:, :o1], w_kv, w_in_ab[:, o3:]], axis=1)
    c_qa = heads_b * 2 * dh
    c_kv = c_qa + q_rank
    c_dk = c_kv + kv_rank + LANES
    c_dv = c_dk + 2 * dh
    assert c_kv % (kv_rank + LANES) == 0 and c_qa % q_rank == 0
    wq = w_q_b.reshape(q_rank, heads_a, nope + rope)
    wq_rope = jnp.pad(wq[:, :, nope:], ((0, 0), (0, 0), (0, LANES - rope)))
    w_q = jnp.concatenate([wq[:, :, :nope].reshape(q_rank, heads_a * nope),
                           wq_rope.reshape(q_rank, heads_a * LANES)], axis=1)
    wuk_t = jnp.transpose(w_uk, (1, 2, 0))
    wuv_t = jnp.transpose(w_uv, (1, 0, 2))
    pos_all = jnp.concatenate([jnp.tile(jnp.arange(t, dtype=jnp.int32), b),
                               jnp.tile(past + jnp.arange(ts, dtype=jnp.int32), db)])
    cosf, sins = _rope_tables(pos_all, rope // 2)
    lam_in = jnp.zeros((8, LANES), F32)
    for i, v in enumerate((lambda_q1, lambda_k1, lambda_q2, lambda_k2)):
        lam_in = lam_in.at[i, :dh].set(v.astype(F32))
    slopes_b = _alibi_slopes(heads_b)
    slopes_c = _alibi_slopes(heads_c)
    t_col = jnp.asarray(np.repeat(np.arange(ts, dtype=np.float32), heads_a).reshape(ts * heads_a, 1))
    slope_col_b = jnp.asarray(np.tile(slopes_b, ts).reshape(ts * heads_b, 1))
    t_col_c = jnp.asarray(np.repeat(np.arange(ts, dtype=np.float32), rh).reshape(ts * rh, 1))
    slope_col_c = jnp.asarray(np.tile(slopes_c.reshape(groups, 1, rh), (1, ts, 1)).reshape(groups, ts * rh, 1))
    pt_flat = page_table.reshape(-1)
    cache_dk = cache_diff_k.reshape(n_pool, page, 2 * dh)
    cache_dv = cache_diff_v.reshape(n_pool, page, dv)
    cache_mk = cache_moba_k.reshape(n_pool, page * groups, d_c)
    cache_mv = cache_moba_v.reshape(n_pool, page * groups, d_c)
    cache_mla_t = jnp.swapaxes(cache_mla, 1, 2)

    n_c = b + db
    c_all = jnp.pad(jnp.concatenate([c_prompt, c_sample], axis=0), ((0, -n_c % 16), (0, 0)))
    mod = _ada(c_all, w_ada, b_ada).reshape(w_ada.shape[0], c_all.shape[0], 6, d_model)
    mods = [(mod[l, :b], mod[l, b:n_c]) for l in range(mod.shape[0])]

    x = jnp.concatenate([x_prompt.reshape(mp, d_model), x_sample.reshape(ms, d_model)], axis=0)
    x = x.reshape(m // ts, ts, d_model)
    pp = functools.partial(_postpre, dims=dims)

    def ffn_mats(h, l):
        a = _matmul(h.reshape(m, d_model), w_ff_up, BF16, bm=1024, bn=512, epilogue="relu2", layer=l)
        return _matmul(a, w_ff_down, F32, bm=1024, bn=1024, bk=2048, layer=l).reshape(x.shape)

    (h,) = pp(x, None, None, None, mods[0], g_pre_mix[0], gate_idx=2, pre_idx=0)
    z0 = _matmul(h.reshape(m, d_model), w0, F32)
    q = _matmul(z0, w_q, F32, a_cols=(c_qa, q_rank), gain=g_q_a)
    qscale_a = (nope + rope) ** -0.5 * LOG2E
    qscale_d = dh ** -0.5 * LOG2E
    q_lat, q_pe = _qhead(q, wuk_t, cosf, sins, heads=heads_a, qscale=qscale_a)
    rows_f, rows_bf = _kvrows(z0, c_kv // (kv_rank + LANES), g_kv_a, cosf, sins, rank=kv_rank)

    ol_p = _mla_prompt(q_lat, q_pe, rows_bf, b=b, t=t, heads=heads_a, rank=kv_rank)
    od_p = _diff_prompt(jnp.asarray(slopes_b), z0, lam_in, g_diff_sub, b=b, t=t, heads=heads_b,
                        k_blk=c_dk // LANES, v_blk=c_dv // LANES, dv=dv, qscale=qscale_d)
    r_ab = ts * heads_a
    ol_s, od_s = _ab_sample(
        pt_flat,
        q_lat[mp:].reshape(db, r_ab, kv_rank), q_pe[mp:].reshape(db, r_ab, LANES),
        z0[mp:, :c_qa].reshape(db, r_ab, 2 * dh),
        _pad_new(rows_bf[mp:], db, ts), _pad_new(z0[mp:, c_dk:c_dv], db, ts), _pad_new(z0[mp:, c_dv:], db, ts),
        slope_col_b, t_col, lam_in, g_diff_sub, cache_mla_t, cache_dk, cache_dv,
        n_pages=n_pages, rank=kv_rank, rope=rope, qscale_d=qscale_d)
    o_lat = jnp.concatenate([ol_p, ol_s.reshape(ms, heads_a * kv_rank)], axis=0)
    o_d = jnp.concatenate([od_p, od_s.reshape(ms, heads_b * dv)], axis=0)
    o_cat = jnp.concatenate([_headmm(o_lat, wuv_t, BF16), o_d], axis=1)
    mix = _matmul(o_cat, w_out_ab, F32, bm=1024, bn=512)
    x, h = pp(x, mix.reshape(x.shape), mods[0], g_post_mix[0], mods[0], g_pre_ffn[0], gate_idx=2, pre_idx=3)
    x, h = pp(x, ffn_mats(h, 0), mods[0], g_post_ffn[0], mods[1], g_pre_mix[1], gate_idx=5, pre_idx=0)

    zc = _matmul(h.reshape(m, d_model), w_in_c, F32)
    nq_c = heads_c * d_c
    qscale_c = d_c ** -0.5 * LOG2E
    oc_p = _moba_prompt(jnp.asarray(slopes_c), zc, b=b, t=t, heads=heads_c, groups=groups, d=d_c,
                        k_blk=heads_c, v_blk=heads_c + groups, qscale=qscale_c)
    q_s = zc[mp:, :nq_c].reshape(db, ts, groups, rh, d_c).transpose(0, 2, 1, 3, 4).reshape(db, groups, ts * rh, d_c)
    oc_s = _moba_sample(pt_flat, q_s, _pad_new(zc[mp:, nq_c:nq_c + groups * d_c], db, ts),
                        _pad_new(zc[mp:, nq_c + groups * d_c:], db, ts), slope_col_c, t_col_c,
                        cache_mk, cache_mv, n_pages=n_pages, groups=groups, d=d_c, qscale=qscale_c)
    oc_s = oc_s.reshape(db, groups, ts, rh, d_c).transpose(0, 2, 1, 3, 4).reshape(ms, nq_c)
    mix = _matmul(jnp.concatenate([oc_p, oc_s], axis=0), w_out_c, F32, bm=1024, bn=512)
    x, h = pp(x, mix.reshape(x.shape), mods[1], g_post_mix[1], mods[1], g_pre_ffn[1], gate_idx=2, pre_idx=3)
    y = ffn_mats(h, 1)
    (y_p,) = pp(x, y, mods[1], g_post_ffn[1], None, None, gate_idx=5, pre_idx=0, part="prompt")
    (y_s,) = pp(x, y, mods[1], g_post_ffn[1], None, None, gate_idx=5, pre_idx=0, part="sample")

    kd = groups * d_c
    return (y_p.reshape(b, t, d_model), y_s.reshape(db, ts, d_model),
            rows_f[:mp, :row_w].reshape(b, t, row_w),
            z0[:mp, c_dk:c_dv].reshape(b, t, 1, 2 * dh), z0[:mp, c_dv:].reshape(b, t, 1, dv),
            zc[:mp, nq_c:nq_c + kd].reshape(b, t, groups, d_c), zc[:mp, nq_c + kd:].reshape(b, t, groups, d_c),
            rows_f[mp:, :row_w].reshape(db, ts, row_w),
            z0[mp:, c_dk:c_dv].reshape(db, ts, 1, 2 * dh), z0[mp:, c_dv:].reshape(db, ts, 1, dv),
            zc[mp:, nq_c:nq_c + kd].reshape(db, ts, groups, d_c), zc[mp:, nq_c + kd:].reshape(db, ts, groups, d_c))
```

```python
import functools
import math

import numpy as np
import jax
import jax.numpy as jnp
from jax import lax
from jax.experimental import pallas as pl
from jax.experimental.pallas import tpu as pltpu

F32 = jnp.float32
BF16 = jnp.bfloat16
EPS = 1e-6
ROPE_THETA = 10000.0
MOBA_BLOCK = 256
BLOCK_SHIFT = MOBA_BLOCK.bit_length() - 1
MOBA_TOPK = 3
DIFF_LAMBDA_INIT = 0.8 - 0.6 * math.exp(-0.3 * 0)
LOG2E = math.log2(math.e)
NEG_INF = float("-inf")
MASKED = -1e30
LANES = 128
SUBLANES = 8
V7X_VMEM_BYTES = 64 * 1024 * 1024
VMEM_LIMIT = V7X_VMEM_BYTES - 8 * 1024 * 1024
NEW_KEY_PAD = 128
Q_TILE = 256
AB_PAGES_PER_STEP = 32
MOBA_PAGES_PER_STEP = 32

def _cp(*sem):
    return pltpu.CompilerParams(dimension_semantics=sem, vmem_limit_bytes=VMEM_LIMIT)


def _pick(n, pref):
    if n <= pref:
        return n
    d = pref - pref % LANES
    while d >= LANES:
        if n % d == 0:
            return d
        d -= LANES
    return n


def _dot(a, b):
    return jnp.dot(a, b, preferred_element_type=F32)


def _dot_nt(a, b):
    return lax.dot_general(a, b, (((1,), (1,)), ((), ())), preferred_element_type=F32)


def _dot3_nt(a, b):
    ah = a.astype(BF16)
    al = (a - ah.astype(F32)).astype(BF16)
    bh = b.astype(BF16)
    bl = (b - bh.astype(F32)).astype(BF16)
    return _dot_nt(ah, bh) + (_dot_nt(ah, bl) + _dot_nt(al, bh))


def _rms(x):
    return x * lax.rsqrt(jnp.mean(x * x, axis=-1, keepdims=True) + EPS)


def _ada_kernel(c_ref, w_ref, b_ref, o_ref):
    c = c_ref[...]
    a = (c * (1.0 / (1.0 + jnp.exp(-c)))).astype(BF16)
    o_ref[0] = _dot(a, w_ref[0].astype(BF16)) + b_ref[0]


def _ada(c_pad, w_ada, b_ada):
    nl, d, n = w_ada.shape
    bc = c_pad.shape[0]
    bn = _pick(n, 512)
    return pl.pallas_call(
        _ada_kernel,
        out_shape=jax.ShapeDtypeStruct((nl, bc, n), F32),
        grid=(nl, n // bn),
        in_specs=[pl.BlockSpec((bc, d), lambda l, j: (0, 0)),
                  pl.BlockSpec((1, d, bn), lambda l, j: (l, 0, j)),
                  pl.BlockSpec((1, 1, bn), lambda l, j: (l, 0, j))],
        out_specs=pl.BlockSpec((1, bc, bn), lambda l, j: (l, 0, j)),
        compiler_params=_cp("arbitrary", "arbitrary"),
        name="ada",
    )(c_pad, w_ada, b_ada.reshape(nl, 1, n))


def _postpre_kernel(*refs, first_sample_blk, has_y, has_h, gate_idx, pre_idx):
    it = iter(refs)
    x_ref = next(it)
    if has_y:
        y_ref, mpp_ref, mps_ref, gpost_ref = next(it), next(it), next(it), next(it)
    if has_h:
        mhp_ref, mhs_ref, gpre_ref = next(it), next(it), next(it)
    if has_y:
        xo_ref = next(it)
    if has_h:
        h_ref = next(it)

    def run(mpost_ref, mpre_ref):
        x = x_ref[...]
        if has_y:
            gate = mpost_ref[:, gate_idx:gate_idx + 1, :]
            x = x + gate * (_rms(y_ref[...]) * gpost_ref[...])
            xo_ref[...] = x
        if has_h:
            shift = mpre_ref[:, pre_idx:pre_idx + 1, :]
            scale = mpre_ref[:, pre_idx + 1:pre_idx + 2, :]
            h_ref[...] = ((_rms(x) * gpre_ref[...]) * (1.0 + scale) + shift).astype(h_ref.dtype)

    i = pl.program_id(0)

    @pl.when(i < first_sample_blk)
    def _():
        run(mpp_ref if has_y else None, mhp_ref if has_h else None)

    @pl.when(i >= first_sample_blk)
    def _():
        run(mps_ref if has_y else None, mhs_ref if has_h else None)


def _postpre(x, y, mod_post, g_post, mod_pre, g_pre, *, dims, gate_idx, pre_idx, part="all"):
    b, t, db, ts = dims
    ng, _, d = x.shape
    gb = max(1, 256 // ts)
    while db % gb or (t // ts) % gb:
        gb //= 2
    bpb = (t // ts) // gb
    nblk_p = b * bpb
    nblk = ng // gb
    lo, hi = {"all": (0, nblk), "prompt": (0, nblk_p), "sample": (nblk_p, nblk)}[part]
    has_y, has_h = y is not None, mod_pre is not None

    xspec = pl.BlockSpec((gb, ts, d), lambda i: (i + lo, 0, 0))
    ospec = pl.BlockSpec((gb, ts, d), lambda i: (i, 0, 0))
    pspec = pl.BlockSpec((1, 6, d), lambda i: (jnp.minimum((i + lo) // bpb, b - 1), 0, 0))
    sspec = pl.BlockSpec((gb, 6, d), lambda i: (jnp.maximum(i + lo - nblk_p, 0), 0, 0))
    gspec = pl.BlockSpec((1, 1, d), lambda i: (0, 0, 0))
    args, specs = [x], [xspec]
    if has_y:
        args += [y, mod_post[0], mod_post[1], g_post.reshape(1, 1, d)]
        specs += [xspec, pspec, sspec, gspec]
    if has_h:
        args += [mod_pre[0], mod_pre[1], g_pre.reshape(1, 1, d)]
        specs += [pspec, sspec, gspec]
    oshape = ((hi - lo) * gb, ts, d)
    out_shape, out_specs = [], []
    if has_y:
        out_shape.append(jax.ShapeDtypeStruct(oshape, F32))
        out_specs.append(ospec)
    if has_h:
        out_shape.append(jax.ShapeDtypeStruct(oshape, BF16))
        out_specs.append(ospec)
    outs = pl.pallas_call(
        functools.partial(_postpre_kernel, first_sample_blk=nblk_p - lo, has_y=has_y, has_h=has_h,
                          gate_idx=gate_idx, pre_idx=pre_idx),
        out_shape=out_shape, grid=(hi - lo,), in_specs=specs, out_specs=out_specs,
        compiler_params=_cp("arbitrary"), name="postpre",
    )(*args)
    return outs


def _mm_kernel(*refs, nk, prologue, epilogue):
    a_ref, w_ref = refs[0], refs[1]
    idx = 2
    if prologue == "rms":
        g_ref = refs[idx]
        idx += 1
    o_ref = refs[idx]
    acc_ref = refs[idx + 1] if nk > 1 else None

    a = a_ref[...]
    if prologue == "rms":
        a = _rms(a.astype(F32)) * g_ref[...]
    w = w_ref[...].reshape(w_ref.shape[-2:]).astype(BF16)

    def finish(v):
        if epilogue == "relu2":
            r = jnp.maximum(v, 0.0)
            v = r * r
        o_ref[...] = v.astype(o_ref.dtype)

    part = _dot(a.astype(BF16), w)
    if nk == 1:
        finish(part)
    else:
        k = pl.program_id(2)

        @pl.when(k == 0)
        def _():
            acc_ref[...] = part

        @pl.when(jnp.logical_and(k > 0, k < nk - 1))
        def _():
            acc_ref[...] += part

        @pl.when(k == nk - 1)
        def _():
            finish(acc_ref[...] + part)


def _matmul(a, w, out_dtype, *, bm=1024, bn=512, bk=None, a_cols=None, gain=None, epilogue=None, layer=None):
    m = a.shape[0]
    k, n = w.shape[-2:]
    bm = _pick(m, bm)
    bn = min(bn, n)
    bk = k if bk is None else _pick(k, bk)
    nk = k // bk
    col0 = 0
    if a_cols is not None:
        assert a_cols[1] == k and a_cols[0] % bk == 0
        col0 = a_cols[0] // bk
    prologue = None
    args = [a, w]
    if layer is None:
        wspec = pl.BlockSpec((bk, bn), lambda i, j, kk: (kk, j))
    else:
        wspec = pl.BlockSpec((1, bk, bn), lambda i, j, kk: (layer, kk, j))
    specs = [pl.BlockSpec((bm, bk), lambda i, j, kk: (i, col0 + kk)), wspec]
    if gain is not None:
        assert nk == 1
        prologue = "rms"
        args.append(gain.reshape(1, k))
        specs.append(pl.BlockSpec((1, k), lambda i, j, kk: (0, 0)))
    return pl.pallas_call(
        functools.partial(_mm_kernel, nk=nk, prologue=prologue, epilogue=epilogue),
        out_shape=jax.ShapeDtypeStruct((m, n), out_dtype),
        grid=(m // bm, pl.cdiv(n, bn), nk),
        in_specs=specs,
        out_specs=pl.BlockSpec((bm, bn), lambda i, j, kk: (i, j)),
        scratch_shapes=[pltpu.VMEM((bm, bn), F32)] if nk > 1 else [],
        compiler_params=_cp("arbitrary", "arbitrary", "arbitrary"),
        name="matmul",
    )(*args)


def _rope128(x, cosf, sins):
    lane = lax.broadcasted_iota(jnp.int32, x.shape, 1)
    first = (lane & 63) < 32
    swapped = jnp.where(first, pltpu.roll(x, 96, 1), pltpu.roll(x, 32, 1))
    return x * cosf + swapped * sins


def _qhead_kernel(qn_ref, qr_ref, wuk_ref, cos_ref, sin_ref, ql_ref, qp_ref, *, qscale):
    ql = _dot(qn_ref[...].astype(BF16), wuk_ref[0].astype(BF16))
    ql_ref[...] = (ql * qscale).astype(ql_ref.dtype)
    qp_ref[...] = (_rope128(qr_ref[...], cos_ref[...], sin_ref[...]) * qscale).astype(qp_ref.dtype)


def _qhead(q, wuk_t, cosf, sins, *, heads, qscale):
    m = q.shape[0]
    _, nope, rank = wuk_t.shape
    bm = _pick(m, 1024)
    return pl.pallas_call(
        functools.partial(_qhead_kernel, qscale=qscale),
        out_shape=[jax.ShapeDtypeStruct((m, heads * rank), BF16),
                   jax.ShapeDtypeStruct((m, heads * LANES), BF16)],
        grid=(m // bm, heads),
        in_specs=[pl.BlockSpec((bm, nope), lambda i, h: (i, h)),
                  pl.BlockSpec((bm, LANES), lambda i, h: (i, heads + h)),
                  pl.BlockSpec((1, nope, rank), lambda i, h: (h, 0, 0)),
                  pl.BlockSpec((bm, LANES), lambda i, h: (i, 0)),
                  pl.BlockSpec((bm, LANES), lambda i, h: (i, 0))],
        out_specs=[pl.BlockSpec((bm, rank), lambda i, h: (i, h)),
                   pl.BlockSpec((bm, LANES), lambda i, h: (i, h))],
        compiler_params=_cp("arbitrary", "arbitrary"),
        name="qhead",
    )(q, q, wuk_t, cosf, sins)


def _headmm_kernel(a_ref, w_ref, o_ref):
    o_ref[...] = _dot(a_ref[...].astype(BF16), w_ref[0].astype(BF16)).astype(o_ref.dtype)


def _headmm(a, w, out_dtype):
    m = a.shape[0]
    heads, k, n = w.shape
    bm = _pick(m, 1024)
    return pl.pallas_call(
        _headmm_kernel,
        out_shape=jax.ShapeDtypeStruct((m, heads * n), out_dtype),
        grid=(m // bm, heads),
        in_specs=[pl.BlockSpec((bm, k), lambda i, h: (i, h)),
                  pl.BlockSpec((1, k, n), lambda i, h: (h, 0, 0))],
        out_specs=pl.BlockSpec((bm, n), lambda i, h: (i, h)),
        compiler_params=_cp("arbitrary", "arbitrary"),
        name="headmm",
    )(a, w)


def _kvrows_kernel(z_ref, g_ref, cos_ref, sin_ref, of_ref, ob_ref, *, rank):
    z = z_ref[...]
    lat = _rms(z[:, :rank]) * g_ref[...]
    pe = _rope128(z[:, rank:], cos_ref[...], sin_ref[...])
    of_ref[:, :rank] = lat
    of_ref[:, rank:] = pe
    ob_ref[:, :rank] = lat.astype(BF16)
    ob_ref[:, rank:] = pe.astype(BF16)


def _kvrows(z, col_blk, g_kv, cosf, sins, *, rank):
    m = z.shape[0]
    w = rank + LANES
    bm = _pick(m, 1024)
    return pl.pallas_call(
        functools.partial(_kvrows_kernel, rank=rank),
        out_shape=[jax.ShapeDtypeStruct((m, w), F32), jax.ShapeDtypeStruct((m, w), BF16)],
        grid=(m // bm,),
        in_specs=[pl.BlockSpec((bm, w), lambda i: (i, col_blk)),
                  pl.BlockSpec((1, rank), lambda i: (0, 0)),
                  pl.BlockSpec((bm, LANES), lambda i: (i, 0)),
                  pl.BlockSpec((bm, LANES), lambda i: (i, 0))],
        out_specs=[pl.BlockSpec((bm, w), lambda i: (i, 0)), pl.BlockSpec((bm, w), lambda i: (i, 0))],
        compiler_params=_cp("arbitrary"),
        name="kvrows",
    )(z, g_kv.reshape(1, rank), cosf, sins)


def _flash_update(s, rc, v, m_ref, l_ref, acc_ref, rows=slice(None), v_transposed=False):
    m_old = m_ref[rows]
    m_new = jnp.maximum(m_old, jnp.max(s, axis=-1, keepdims=True) + rc)
    alpha = jnp.exp2(m_old - m_new)
    p = jnp.exp2(s - (m_new - rc))
    l_ref[rows] = alpha * l_ref[rows] + jnp.sum(p, axis=-1, keepdims=True)
    pb = p.astype(BF16)
    acc_ref[rows] = alpha * acc_ref[rows] + (_dot_nt(pb, v) if v_transposed else _dot(pb, v))
    m_ref[rows] = m_new


def _flash_init(m_ref, l_ref, acc_ref):
    m_ref[...] = jnp.full(m_ref.shape, NEG_INF, F32)
    l_ref[...] = jnp.zeros(l_ref.shape, F32)
    acc_ref[...] = jnp.zeros(acc_ref.shape, F32)


def _lambda_value(lam_ref):
    lv = lam_ref[...]
    d1 = jnp.sum(lv[0:1] * lv[1:2], axis=-1, keepdims=True)
    d2 = jnp.sum(lv[2:3] * lv[3:4], axis=-1, keepdims=True)
    return jnp.exp(d1) - jnp.exp(d2) + DIFF_LAMBDA_INIT


def _split_halves(q):
    lane = lax.broadcasted_iota(jnp.int32, q.shape, 1)
    lo = lane < (q.shape[1] // 2)
    zero = jnp.zeros_like(q)
    return jnp.where(lo, q, zero), jnp.where(lo, zero, q)


def _diff_finish(acc1, l1, acc2, l2, lam, g):
    a = acc1 / l1 - lam * (acc2 / l2)
    return (_rms(a) * g) * (1.0 - DIFF_LAMBDA_INIT)


def _flash_update_t(st, rc, vt, m_ref, l_ref, acc_ref, cols=slice(None)):
    m_old = m_ref[:, cols]
    m_new = jnp.maximum(m_old, jnp.max(st, axis=0, keepdims=True) + rc)
    alpha = jnp.exp2(m_old - m_new)
    p = jnp.exp2(st - (m_new - rc))
    l_ref[:, cols] = alpha * l_ref[:, cols] + jnp.sum(p, axis=0, keepdims=True)
    acc_ref[:, cols] = alpha * acc_ref[:, cols] + _dot(vt, p.astype(BF16))
    m_ref[:, cols] = m_new


def _causal_keep_t(tq, cols):
    key = lax.broadcasted_iota(jnp.int32, (tq, cols), 0)
    pos = lax.broadcasted_iota(jnp.int32, (tq, cols), 1) & (tq - 1)
    return key <= pos


def _rel_bias_t(tq):
    key = lax.broadcasted_iota(jnp.int32, (tq, tq), 0)
    pos = lax.broadcasted_iota(jnp.int32, (tq, tq), 1)
    return (key - pos).astype(F32)


def _transpose_bf16(x):
    return x.astype(F32).T.astype(BF16)


def _mla_prompt_kernel(ql_ref, qp_ref, k_ref, o_ref, qs, kt, m_ref, l_ref, acc_ref, *, tq, nq, heads, rank):
    qi = pl.program_id(1)

    @pl.when(qi == 0)
    def _():
        for c in range(nq):
            kt[c] = _transpose_bf16(k_ref[c * tq:(c + 1) * tq, :rank])

    for h in range(heads):
        rows = slice(h * tq, (h + 1) * tq)
        qs[rows, :rank] = ql_ref[:, h * rank:(h + 1) * rank]
        qs[rows, rank:] = qp_ref[:, h * LANES:(h + 1) * LANES]
    _flash_init(m_ref, l_ref, acc_ref)

    def body(c, carry):
        kc = k_ref[pl.ds(pl.multiple_of(c * tq, tq), tq), :]
        _flash_update_t(_dot_nt(kc, qs[...]), 0.0, kt[c], m_ref, l_ref, acc_ref)
        return carry

    lax.fori_loop(0, qi, body, 0)
    kc = k_ref[pl.ds(pl.multiple_of(qi * tq, tq), tq), :]
    st = jnp.where(_causal_keep_t(tq, heads * tq), _dot_nt(kc, qs[...]), NEG_INF)
    _flash_update_t(st, 0.0, kt[qi], m_ref, l_ref, acc_ref)
    for h in range(heads):
        cols = slice(h * tq, (h + 1) * tq)
        o_ref[:, h * rank:(h + 1) * rank] = (acc_ref[:, cols] / l_ref[:, cols]).T.astype(o_ref.dtype)


def _mla_prompt(q_lat, q_pe, rows_bf, *, b, t, heads, rank):
    tq = Q_TILE
    assert t % tq == 0
    nq = t // tq
    w = rank + LANES
    r = heads * tq
    return pl.pallas_call(
        functools.partial(_mla_prompt_kernel, tq=tq, nq=nq, heads=heads, rank=rank),
        out_shape=jax.ShapeDtypeStruct((b * t, heads * rank), BF16),
        grid=(b, nq),
        in_specs=[pl.BlockSpec((tq, heads * rank), lambda bb, i: (bb * nq + i, 0)),
                  pl.BlockSpec((tq, heads * LANES), lambda bb, i: (bb * nq + i, 0)),
                  pl.BlockSpec((t, w), lambda bb, i: (bb, 0))],
        out_specs=pl.BlockSpec((tq, heads * rank), lambda bb, i: (bb * nq + i, 0)),
        scratch_shapes=[pltpu.VMEM((r, w), BF16), pltpu.VMEM((nq, rank, tq), BF16),
                        pltpu.VMEM((1, r), F32), pltpu.VMEM((1, r), F32), pltpu.VMEM((rank, r), F32)],
        compiler_params=_cp("arbitrary", "arbitrary"),
        name="mla_prompt",
    )(q_lat, q_pe, rows_bf)


def _diff_prompt_kernel(slopes_ref, q_ref, k_ref, v_ref, lam_ref, g_ref, o_ref,
                        qs, kb, vt, relb, slr, m_ref, l_ref, acc_ref, *, tq, nq, heads, dv, qscale):
    qi = pl.program_id(1)
    hr = heads * tq

    @pl.when(qi == 0)
    def _():
        kb[...] = k_ref[...].astype(BF16)
        for c in range(nq):
            vt[c] = v_ref[c * tq:(c + 1) * tq, :].T.astype(BF16)
        rel = _rel_bias_t(tq)
        for h in range(heads):
            s2 = slopes_ref[h] * LOG2E
            relb[:, h * tq:(h + 1) * tq] = s2 * rel
            slr[:, h * tq:(h + 1) * tq] = jnp.full((1, tq), s2, F32)

    for h in range(heads):
        q1, q2 = _split_halves((q_ref[:, h * LANES:(h + 1) * LANES] * qscale).astype(BF16))
        qs[h * tq:(h + 1) * tq, :] = q1
        qs[hr + h * tq:hr + (h + 1) * tq, :] = q2
    _flash_init(m_ref, l_ref, acc_ref)
    halves = (slice(0, hr), slice(hr, 2 * hr))

    def body(c, carry):
        k = kb[pl.ds(pl.multiple_of(c * tq, tq), tq), :]
        rc = slr[...] * ((c - qi) * tq).astype(F32)
        for half in halves:
            _flash_update_t(_dot_nt(k, qs[half]) + relb[...], rc, vt[c], m_ref, l_ref, acc_ref, half)
        return carry

    lax.fori_loop(0, qi, body, 0)
    k = kb[pl.ds(pl.multiple_of(qi * tq, tq), tq), :]
    keep = _causal_keep_t(tq, hr)
    for half in halves:
        st = jnp.where(keep, _dot_nt(k, qs[half]) + relb[...], NEG_INF)
        _flash_update_t(st, 0.0, vt[qi], m_ref, l_ref, acc_ref, half)

    lam = _lambda_value(lam_ref)
    for h in range(heads):
        c1 = slice(h * tq, (h + 1) * tq)
        c2 = slice(hr + h * tq, hr + (h + 1) * tq)
        a = acc_ref[:, c1] / l_ref[:, c1] - lam * (acc_ref[:, c2] / l_ref[:, c2])
        o_ref[:, h * dv:(h + 1) * dv] = (
            (_rms(a.T) * g_ref[...]) * (1.0 - DIFF_LAMBDA_INIT)).astype(o_ref.dtype)


def _diff_prompt(slopes, z0, lam_in, g_sub, *, b, t, heads, k_blk, v_blk, dv, qscale):
    tq = Q_TILE
    assert t % tq == 0
    nq = t // tq
    hr = heads * tq
    return pl.pallas_call(
        functools.partial(_diff_prompt_kernel, tq=tq, nq=nq, heads=heads, dv=dv, qscale=qscale),
        out_shape=jax.ShapeDtypeStruct((b * t, heads * dv), BF16),
        grid=(b, nq),
        in_specs=[pl.BlockSpec(memory_space=pltpu.SMEM),
                  pl.BlockSpec((tq, heads * LANES), lambda bb, i: (bb * nq + i, 0)),
                  pl.BlockSpec((t, LANES), lambda bb, i: (bb, k_blk)),
                  pl.BlockSpec((t, dv), lambda bb, i: (bb, v_blk)),
                  pl.BlockSpec((8, LANES), lambda bb, i: (0, 0)),
                  pl.BlockSpec((1, dv), lambda bb, i: (0, 0))],
        out_specs=pl.BlockSpec((tq, heads * dv), lambda bb, i: (bb * nq + i, 0)),
        scratch_shapes=[pltpu.VMEM((2 * hr, LANES), BF16), pltpu.VMEM((t, LANES), BF16),
                        pltpu.VMEM((nq, dv, tq), BF16), pltpu.VMEM((tq, hr), F32), pltpu.VMEM((1, hr), F32),
                        pltpu.VMEM((1, 2 * hr), F32), pltpu.VMEM((1, 2 * hr), F32),
                        pltpu.VMEM((dv, 2 * hr), F32)],
        compiler_params=_cp("arbitrary", "arbitrary"),
        name="diff_prompt",
    )(slopes, z0, z0, z0, lam_in, g_sub.reshape(1, dv))


def _ab_sample_kernel(pt_ref, ql_ref, qp_ref, qd_ref, kn_ref, dkn_ref, dvn_ref, slope_ref, t_ref,
                      tab_ref, lam_ref, g_ref, cm_hbm, ck_hbm, cv_hbm, ol_ref, od_ref,
                      pg_m, pg_k, pg_v, sem, kt, dkb, dvb, m_a, l_a, acc_a, m_d, l_d, acc_d,
                      *, npg, n_pages, n_seq, page, rank, rope, past, nchunk, qscale_d):
    b = pl.program_id(0)
    j = pl.program_id(1)
    step = b * nchunk + j
    slot = step & 1
    kc = npg * page
    row_w = rank + rope

    def page_copies(page_id, slt, p):
        return (pltpu.make_async_copy(cm_hbm.at[page_id], pg_m.at[slt, p], sem.at[slt]),
                pltpu.make_async_copy(ck_hbm.at[page_id], pg_k.at[slt, p], sem.at[slt]),
                pltpu.make_async_copy(cv_hbm.at[page_id], pg_v.at[slt, p], sem.at[slt]))

    def start_gather(bb, jj, slt):
        for p in range(npg):
            for cp in page_copies(pt_ref[bb * n_pages + jj * npg + p], slt, p):
                cp.start()

    @pl.when(step == 0)
    def _():
        start_gather(b, j, slot)

    @pl.when(step + 1 < n_seq * nchunk)
    def _():
        last = j == nchunk - 1
        start_gather(jnp.where(last, b + 1, b), jnp.where(last, 0, j + 1), 1 - slot)

    for p in range(npg):
        for cp in page_copies(0, slot, p):
            cp.wait()

    @pl.when(j == 0)
    def _():
        _flash_init(m_a, l_a, acc_a)
        _flash_init(m_d, l_d, acc_d)
        kt[row_w:, :] = jnp.zeros((kt.shape[0] - row_w, kc), BF16)

    for p in range(npg):
        cols = slice(p * page, (p + 1) * page)
        kt[:row_w, cols] = pg_m[slot, p].astype(BF16)
        dkb[cols, :] = pg_k[slot, p].astype(BF16)
        dvb[cols, :] = pg_v[slot, p].astype(BF16)

    ql = ql_ref[0]
    qp = qp_ref[0]
    q1, q2 = _split_halves((qd_ref[0] * qscale_d).astype(BF16))
    qd = jnp.concatenate([q1, q2], axis=0)
    slope2 = slope_ref[...] * LOG2E
    tq = t_ref[...]
    r = ql.shape[0]

    rc = slope2 * ((j * kc - past).astype(F32) - tq)
    lat_t = kt[:rank, :]
    _flash_update(_dot(ql, lat_t) + _dot(qp, kt[rank:, :]), 0.0, lat_t, m_a, l_a, acc_a, v_transposed=True)
    _flash_update(_dot_nt(qd, dkb[...]) + tab_ref[...], rc, dvb[...], m_d, l_d, acc_d)

    @pl.when(j == nchunk - 1)
    def _():
        kn = kn_ref[0]
        nk = kn.shape[0]
        tk = lax.broadcasted_iota(jnp.int32, (2 * r, nk), 1).astype(F32)
        keep = tk <= tq
        latn = kn[:, :rank]
        sn = _dot_nt(ql, latn) + _dot_nt(qp, kn[:, rank:])
        _flash_update(jnp.where(keep[:r], sn, NEG_INF), 0.0, latn, m_a, l_a, acc_a)
        sd = _dot_nt(qd, dkn_ref[0]) + slope2 * (tk - tq)
        _flash_update(jnp.where(keep, sd, NEG_INF), 0.0, dvn_ref[0], m_d, l_d, acc_d)
        ol_ref[0] = (acc_a[...] / l_a[...]).astype(ol_ref.dtype)
        od_ref[0] = _diff_finish(acc_d[:r], l_d[:r], acc_d[r:], l_d[r:],
                                 _lambda_value(lam_ref), g_ref[...]).astype(od_ref.dtype)


def _pages_per_step(n_pages, most):
    for p in (32, 16, 8, 4, 2, 1):
        if p > most:
            continue
        if n_pages % p == 0 and n_pages // p >= 2:
            return p
    return 1


def _ab_sample(pt_flat, ql, qp, qd, kn, dkn, dvn, slope_col, t_col, lam_in, g_sub,
               cache_mla, cache_dk, cache_dv, *, n_pages, rank, rope, qscale_d):
    db, r, _ = ql.shape
    row_w, page = cache_mla.shape[1:]
    dv = cache_dv.shape[2]
    npg = _pages_per_step(n_pages, AB_PAGES_PER_STEP)
    nchunk = n_pages // npg
    kc = npg * page
    past = n_pages * page
    slope2 = jnp.concatenate([slope_col, slope_col], axis=0)
    t2 = jnp.concatenate([t_col, t_col], axis=0)
    tab = (slope2 * LOG2E) * jnp.arange(kc, dtype=F32)[None, :]

    def seq_spec(shape):
        return pl.BlockSpec((1,) + shape, lambda b, j, pt: (b, 0, 0))

    def const_spec(shape):
        return pl.BlockSpec(shape, lambda b, j, pt: (0, 0))

    in_specs = [seq_spec((r, rank)), seq_spec((r, LANES)), seq_spec((r, LANES)),
                seq_spec((NEW_KEY_PAD, rank + LANES)), seq_spec((NEW_KEY_PAD, LANES)),
                seq_spec((NEW_KEY_PAD, dv)),
                const_spec((2 * r, 1)), const_spec((2 * r, 1)),
                pl.BlockSpec((2 * r, kc), lambda b, j, pt: (0, 0), pipeline_mode=pl.Buffered(1)),
                const_spec((8, LANES)), const_spec((1, dv))]
    in_specs += [pl.BlockSpec(memory_space=pl.ANY)] * 3
    scratch = [pltpu.VMEM((2, npg, row_w, page), F32), pltpu.VMEM((2, npg, page, LANES), F32),
               pltpu.VMEM((2, npg, page, dv), F32), pltpu.SemaphoreType.DMA((2,)),
               pltpu.VMEM((rank + LANES, kc), BF16),
               pltpu.VMEM((kc, LANES), BF16), pltpu.VMEM((kc, dv), BF16),
               pltpu.VMEM((r, 1), F32), pltpu.VMEM((r, 1), F32), pltpu.VMEM((r, rank), F32),
               pltpu.VMEM((2 * r, 1), F32), pltpu.VMEM((2 * r, 1), F32), pltpu.VMEM((2 * r, dv), F32)]
    return pl.pallas_call(
        functools.partial(_ab_sample_kernel, npg=npg, n_pages=n_pages, n_seq=db, page=page, rank=rank,
                          rope=rope, past=past, nchunk=nchunk, qscale_d=qscale_d),
        out_shape=[jax.ShapeDtypeStruct((db, r, rank), BF16), jax.ShapeDtypeStruct((db, r, dv), BF16)],
        grid_spec=pltpu.PrefetchScalarGridSpec(
            num_scalar_prefetch=1, grid=(db, nchunk), in_specs=in_specs,
            out_specs=[seq_spec((r, rank)), seq_spec((r, dv))], scratch_shapes=scratch),
        compiler_params=_cp("arbitrary", "arbitrary"),
        name="ab_sample",
    )(pt_flat, ql, qp, qd, kn, dkn, dvn, slope2, t2, tab, lam_in, g_sub.reshape(1, dv),
      cache_mla, cache_dk, cache_dv)


def _topk_mask_t(gate_ref, nblk, n_valid, n_sel, always):
    del nblk
    blk = lax.broadcasted_iota(jnp.int32, gate_ref.shape, 0).astype(F32)
    valid = blk < float(n_valid)
    g = jnp.where(valid, gate_ref[...], NEG_INF)
    chosen = blk == float(always)
    for _ in range(n_sel):
        is_max = g == jnp.max(g, axis=0, keepdims=True)
        first = jnp.min(jnp.where(is_max, blk, float(g.shape[0])), axis=0, keepdims=True)
        pick = jnp.logical_and(blk == first, valid)
        chosen = jnp.logical_or(chosen, pick)
        g = jnp.where(pick, NEG_INF, g)
    return jnp.where(chosen, 0.0, MASKED)


def _moba_prompt_kernel(slopes_ref, q_ref, k_ref, v_ref, o_ref,
                        qs, kaug, vt, kmean, relb, slr, gsc, m_ref, l_ref, acc_ref,
                        *, blk, nblk, rh, d, qscale):
    g = pl.program_id(1)
    qi = pl.program_id(2)
    hr = rh * blk

    @pl.when(qi == 0)
    def _():
        t = k_ref.shape[0]
        kaug[:, :d] = k_ref[...].astype(BF16)
        krow_blk = lax.broadcasted_iota(jnp.int32, (t, d), 0) >> BLOCK_SHIFT
        lane = lax.broadcasted_iota(jnp.int32, (t, d), 1)
        kaug[:, d:] = jnp.where((lane & 7) == krow_blk, 1.0, 0.0).astype(BF16)
        kmean[...] = jnp.zeros(kmean.shape, F32)
        for n in range(nblk):
            vt[n] = v_ref[n * blk:(n + 1) * blk, :].T.astype(BF16)
            kmean[n:n + 1, :] = jnp.mean(k_ref[n * blk:(n + 1) * blk, :], axis=0, keepdims=True)
        rel = _rel_bias_t(blk)
        for h in range(rh):
            s2 = slopes_ref[g * rh + h] * LOG2E
            relb[:, h * blk:(h + 1) * blk] = s2 * rel
            slr[:, h * blk:(h + 1) * blk] = jnp.full((1, blk), s2, F32)

    km = kmean[...]
    for h in range(rh):
        qh = q_ref[:, h * d:(h + 1) * d]
        qs[h * blk:(h + 1) * blk, :d] = (qh * qscale).astype(BF16)
        gsc[h * SUBLANES:(h + 1) * SUBLANES, :] = _dot3_nt(km, qh)

    gate = gsc[...].reshape(rh, SUBLANES, blk)
    bidx = lax.broadcasted_iota(jnp.int32, gate.shape, 1)
    rank = jnp.zeros(gate.shape, F32)
    for k in range(nblk):
        gk = gate[:, k:k + 1, :]
        beats = jnp.logical_or(gk > gate, jnp.logical_and(gk == gate, k < bidx))
        rank = rank + jnp.where(jnp.logical_and(beats, k < qi), 1.0, 0.0)
    chosen = jnp.logical_and(bidx < qi, rank < float(min(MOBA_TOPK, nblk - 1)))
    sel = jnp.where(jnp.logical_or(chosen, bidx == qi), 0.0, MASKED)
    sel_rows = sel.reshape(rh * SUBLANES, blk).T.astype(BF16)
    lane_head = lax.broadcasted_iota(jnp.int32, sel_rows.shape, 1) >> 3
    zero = jnp.zeros_like(sel_rows)
    for h in range(rh):
        qs[h * blk:(h + 1) * blk, d:] = jnp.where(lane_head == h, sel_rows, zero)

    _flash_init(m_ref, l_ref, acc_ref)
    own = pl.multiple_of(qi * blk, blk)
    st = jnp.where(_causal_keep_t(blk, hr), _dot_nt(kaug[pl.ds(own, blk), :], qs[...]) + relb[...], NEG_INF)
    _flash_update_t(st, 0.0, vt[qi], m_ref, l_ref, acc_ref)

    def body(jb, carry):
        off = pl.multiple_of(jb * blk, blk)
        rc = slr[...] * ((jb - qi) * blk).astype(F32)
        _flash_update_t(_dot_nt(kaug[pl.ds(off, blk), :], qs[...]) + relb[...], rc, vt[jb],
                        m_ref, l_ref, acc_ref)
        return carry

    lax.fori_loop(0, qi, body, 0)
    for h in range(rh):
        cols = slice(h * blk, (h + 1) * blk)
        o_ref[:, h * d:(h + 1) * d] = (acc_ref[:, cols] / l_ref[:, cols]).T.astype(o_ref.dtype)


def _moba_prompt(slopes, zc, *, b, t, heads, groups, d, k_blk, v_blk, qscale):
    blk = MOBA_BLOCK
    assert t % blk == 0
    nblk = t // blk
    rh = heads // groups
    assert nblk <= SUBLANES and rh * SUBLANES == LANES and d == LANES
    hr = rh * blk
    return pl.pallas_call(
        functools.partial(_moba_prompt_kernel, blk=blk, nblk=nblk, rh=rh, d=d, qscale=qscale),
        out_shape=jax.ShapeDtypeStruct((b * t, heads * d), BF16),
        grid=(b, groups, nblk),
        in_specs=[pl.BlockSpec(memory_space=pltpu.SMEM),
                  pl.BlockSpec((blk, rh * d), lambda bb, g, i: (bb * nblk + i, g)),
                  pl.BlockSpec((t, d), lambda bb, g, i: (bb, k_blk + g)),
                  pl.BlockSpec((t, d), lambda bb, g, i: (bb, v_blk + g))],
        out_specs=pl.BlockSpec((blk, rh * d), lambda bb, g, i: (bb * nblk + i, g)),
        scratch_shapes=[pltpu.VMEM((hr, 2 * d), BF16), pltpu.VMEM((t, 2 * d), BF16),
                        pltpu.VMEM((nblk, d, blk), BF16),
                        pltpu.VMEM((SUBLANES, d), F32), pltpu.VMEM((blk, hr), F32), pltpu.VMEM((1, hr), F32),
                        pltpu.VMEM((rh * SUBLANES, blk), F32),
                        pltpu.VMEM((1, hr), F32), pltpu.VMEM((1, hr), F32), pltpu.VMEM((d, hr), F32)],
        compiler_params=_cp("arbitrary", "arbitrary", "arbitrary"),
        name="moba_prompt",
    )(slopes, zc, zc, zc)


def _moba_sample_kernel(pt_ref, q_ref, kn_ref, vn_ref, slope_ref, t_ref, tab_ref, ck_hbm, cv_hbm, o_ref,
                        pages, sem, s_all, kb, vb, ksum, gsc, selb, m_s, l_s, acc_s,
                        *, npg, n_pages, n_seq, page, groups, d, past, nchunk, nfull, qscale):
    b = pl.program_id(0)
    j = pl.program_id(1)
    nsteps = 2 * nchunk
    step = b * nsteps + j
    slot = step & 1
    kc = npg * page
    ppb = MOBA_BLOCK // page
    bpc = kc // MOBA_BLOCK
    r = q_ref.shape[2]
    tq = t_ref[...]

    def page_copy(cache, page_id, slt, p):
        return pltpu.make_async_copy(cache.at[page_id], pages.at[slt, p], sem.at[slt])

    def start_gather(bb, jj, slt):
        @pl.when(jj < nchunk)
        def _():
            for p in range(npg):
                page_copy(ck_hbm, pt_ref[bb * n_pages + jj * npg + p], slt, p).start()

        @pl.when(jj >= nchunk)
        def _():
            for p in range(npg):
                page_copy(cv_hbm, pt_ref[bb * n_pages + (jj - nchunk) * npg + p], slt, p).start()

    @pl.when(step == 0)
    def _():
        start_gather(b, j, slot)

    @pl.when(step + 1 < n_seq * nsteps)
    def _():
        last = j == nsteps - 1
        start_gather(jnp.where(last, b + 1, b), jnp.where(last, 0, j + 1), 1 - slot)

    for p in range(npg):
        page_copy(ck_hbm, 0, slot, p).wait()

    @pl.when(j == 0)
    def _():
        ksum[...] = jnp.zeros(ksum.shape, F32)

    @pl.when(j < nchunk)
    def _():
        for g in range(groups):
            csum = None
            for p in range(npg):
                x = pages[slot, p, pl.ds(g, page, stride=groups), :]
                kb[p * page:(p + 1) * page, g * d:(g + 1) * d] = x.astype(BF16)
                cs = jnp.sum(x, axis=0, keepdims=True)
                csum = cs if p % ppb == 0 else csum + cs
                if p % ppb == ppb - 1:
                    ksum[g, pl.ds(j * bpc + p // ppb, 1), :] = csum
        for g in range(groups):
            qb = (q_ref[0, g] * qscale).astype(BF16)
            s_all[g, j] = _dot_nt(qb, kb[:, g * d:(g + 1) * d]) + tab_ref[g]

    @pl.when(j == nchunk - 1)
    def _():
        kn = kn_ref[0]
        vn = vn_ref[0]
        nk = kn.shape[0]
        tk = lax.broadcasted_iota(jnp.int32, (r, nk), 1).astype(F32)
        keep = tk <= tq
        for g in range(groups):
            q = q_ref[0, g]
            nb = gsc.shape[0]
            gsc[...] = _dot3_nt(ksum[g, :nb] * (1.0 / MOBA_BLOCK), q)
            mask = _topk_mask_t(gsc, nfull, nfull, min(MOBA_TOPK, nfull), -1)
            if nb < LANES:
                mask = jnp.concatenate([mask, jnp.zeros((LANES - nb, mask.shape[1]), F32)], axis=0)
            selb[g] = mask.T.astype(BF16)
            m_ref, l_ref, acc_ref = m_s.at[g], l_s.at[g], acc_s.at[g]
            _flash_init(m_ref, l_ref, acc_ref)
            sn = _dot_nt((q * qscale).astype(BF16), kn[:, g * d:(g + 1) * d]) + (slope_ref[g] * LOG2E) * (tk - tq)
            _flash_update(jnp.where(keep, sn, NEG_INF), 0.0, vn[:, g * d:(g + 1) * d], m_ref, l_ref, acc_ref)

    @pl.when(j >= nchunk)
    def _():
        c = j - nchunk
        for g in range(groups):
            for p in range(npg):
                vb[p * page:(p + 1) * page, g * d:(g + 1) * d] = (
                    pages[slot, p, pl.ds(g, page, stride=groups), :].astype(BF16))
        kblk = c * bpc + (lax.broadcasted_iota(jnp.int32, (LANES, kc), 1) >> BLOCK_SHIFT)
        ind = jnp.where(lax.broadcasted_iota(jnp.int32, (LANES, kc), 0) == kblk, 1.0, 0.0).astype(BF16)
        for g in range(groups):
            rc = (slope_ref[g] * LOG2E) * ((c * kc - past).astype(F32) - tq)
            _flash_update(s_all[g, c] + _dot(selb[g], ind), rc, vb[:, g * d:(g + 1) * d],
                          m_s.at[g], l_s.at[g], acc_s.at[g])

    @pl.when(j == 2 * nchunk - 1)
    def _():
        for g in range(groups):
            o_ref[0, g] = (acc_s[g] / l_s[g]).astype(o_ref.dtype)


def _moba_sample(pt_flat, q, kn, vn, slope_col, t_col, cache_k, cache_v, *, n_pages, groups, d, qscale):
    db, _, r, _ = q.shape
    page = cache_k.shape[1] // groups
    assert MOBA_BLOCK % page == 0
    past = n_pages * page
    assert past % MOBA_BLOCK == 0
    nfull = past // MOBA_BLOCK
    assert 1 <= nfull <= LANES
    npg = _pages_per_step(n_pages, MOBA_PAGES_PER_STEP)
    while (npg * page) % MOBA_BLOCK:
        npg *= 2
    nchunk = n_pages // npg
    kc = npg * page
    gd = groups * d
    tab = (slope_col * LOG2E) * jnp.arange(kc, dtype=F32)[None, None, :]

    in_specs = [pl.BlockSpec((1, groups, r, d), lambda b, j, pt: (b, 0, 0, 0)),
                pl.BlockSpec((1, NEW_KEY_PAD, gd), lambda b, j, pt: (b, 0, 0)),
                pl.BlockSpec((1, NEW_KEY_PAD, gd), lambda b, j, pt: (b, 0, 0)),
                pl.BlockSpec((groups, r, 1), lambda b, j, pt: (0, 0, 0)),
                pl.BlockSpec((r, 1), lambda b, j, pt: (0, 0)),
                pl.BlockSpec((groups, r, kc), lambda b, j, pt: (0, 0, 0), pipeline_mode=pl.Buffered(1)),
                pl.BlockSpec(memory_space=pl.ANY), pl.BlockSpec(memory_space=pl.ANY)]
    scratch = [pltpu.VMEM((2, npg, page * groups, d), F32), pltpu.SemaphoreType.DMA((2,)),
               pltpu.VMEM((groups, nchunk, r, kc), F32),
               pltpu.VMEM((kc, gd), BF16), pltpu.VMEM((kc, gd), BF16),
               pltpu.VMEM((groups, LANES, d), F32), pltpu.VMEM((-(-nfull // SUBLANES) * SUBLANES, r), F32),
               pltpu.VMEM((groups, r, LANES), BF16),
               pltpu.VMEM((groups, r, 1), F32), pltpu.VMEM((groups, r, 1), F32),
               pltpu.VMEM((groups, r, d), F32)]
    return pl.pallas_call(
        functools.partial(_moba_sample_kernel, npg=npg, n_pages=n_pages, n_seq=db, page=page, groups=groups,
                          d=d, past=past, nchunk=nchunk, nfull=nfull, qscale=qscale),
        out_shape=jax.ShapeDtypeStruct((db, groups, r, d), BF16),
        grid_spec=pltpu.PrefetchScalarGridSpec(
            num_scalar_prefetch=1, grid=(db, 2 * nchunk), in_specs=in_specs,
            out_specs=pl.BlockSpec((1, groups, r, d), lambda b, j, pt: (b, 0, 0, 0)),
            scratch_shapes=scratch),
        compiler_params=_cp("arbitrary", "arbitrary"),
        name="moba_sample",
    )(pt_flat, q, kn, vn, slope_col, t_col, tab, cache_k, cache_v)


def _alibi_slopes(n):
    return np.asarray(2.0 ** (-8.0 * np.arange(1, n + 1) / n), dtype=np.float32)


def _rope_tables(pos, half):
    inv = ROPE_THETA ** (-jnp.arange(half, dtype=F32) / half)
    ang = pos.astype(F32)[:, None] * inv
    cos, sin = jnp.cos(ang), jnp.sin(ang)
    reps = LANES // (2 * half)
    return jnp.tile(cos, (1, 2 * reps)), jnp.tile(jnp.concatenate([-sin, sin], axis=-1), (1, reps))


def _pad_new(x, db, ts):
    x = x.reshape(db, ts, x.shape[-1]).astype(BF16)
    return jnp.pad(x, ((0, 0), (0, NEW_KEY_PAD - ts), (0, 0)))


def kernel(x_prompt, x_sample, cache_mla, cache_diff_k, cache_diff_v, cache_moba_k, cache_moba_v,
           page_table, c_prompt, c_sample, w_ada, b_ada, g_pre_mix, g_post_mix, g_pre_ffn, g_post_ffn,
           w_in_ab, g_q_a, w_q_b, g_kv_a, w_uk, w_uv, lambda_q1, lambda_k1, lambda_q2, lambda_k2,
           g_diff_sub, w_out_ab, w_in_c, w_out_c, w_ff_up, w_ff_down):
    b, t, d_model = x_prompt.shape
    db, ts, _ = x_sample.shape
    mp, ms = b * t, db * ts
    m = mp + ms
    dims = (b, t, db, ts)
    n_pool, page, row_w = cache_mla.shape
    n_pages = page_table.shape[1]
    past = n_pages * page
    q_rank = g_q_a.shape[0]
    kv_rank, heads_a, nope = w_uk.shape
    rope = row_w - kv_rank
    dh = lambda_q1.shape[0]
    dv = g_diff_sub.shape[0]
    heads_b = (w_in_ab.shape[1] - q_rank - row_w - 2 * dh - dv) // (2 * dh)
    groups, d_c = cache_moba_k.shape[2], cache_moba_k.shape[3]
    heads_c = w_out_c.shape[0] // d_c
    rh = heads_c // groups
    assert nope == LANES and 2 * rope == LANES and 2 * dh == LANES and dv == LANES and d_c == LANES
    assert kv_rank % LANES == 0 and heads_a == heads_b and cache_diff_k.shape[2] == 1
    assert ts <= NEW_KEY_PAD

    o1, o2 = q_rank, q_rank + row_w
    o3 = o2 + heads_b * 2 * dh
    w_kv = jnp.pad(w_in_ab[:, o1:o2], ((0, 0), (0, LANES - rope)))
    w0 = jnp.concatenate([w_in_ab[:, o2:o3], w_in_ab[:, :o1], w_kv, w_in_ab[:, o3:]], axis=1)
    c_qa = heads_b * 2 * dh
    c_kv = c_qa + q_rank
    c_dk = c_kv + kv_rank + LANES
    c_dv = c_dk + 2 * dh
    assert c_kv % (kv_rank + LANES) == 0 and c_qa % q_rank == 0
    wq = w_q_b.reshape(q_rank, heads_a, nope + rope)
    wq_rope = jnp.pad(wq[:, :, nope:], ((0, 0), (0, 0), (0, LANES - rope)))
    w_q = jnp.concatenate([wq[:, :, :nope].reshape(q_rank, heads_a * nope),
                           wq_rope.reshape(q_rank, heads_a * LANES)], axis=1)
    wuk_t = jnp.transpose(w_uk, (1, 2, 0))
    wuv_t = jnp.transpose(w_uv, (1, 0, 2))
    pos_all = jnp.concatenate([jnp.tile(jnp.arange(t, dtype=jnp.int32), b),
                               jnp.tile(past + jnp.arange(ts, dtype=jnp.int32), db)])
    cosf, sins = _rope_tables(pos_all, rope // 2)
    lam_in = jnp.zeros((8, LANES), F32)
    for i, v in enumerate((lambda_q1, lambda_k1, lambda_q2, lambda_k2)):
        lam_in = lam_in.at[i, :dh].set(v.astype(F32))
    slopes_b = _alibi_slopes(heads_b)
    slopes_c = _alibi_slopes(heads_c)
    t_col = jnp.asarray(np.repeat(np.arange(ts, dtype=np.float32), heads_a).reshape(ts * heads_a, 1))
    slope_col_b = jnp.asarray(np.tile(slopes_b, ts).reshape(ts * heads_b, 1))
    t_col_c = jnp.asarray(np.repeat(np.arange(ts, dtype=np.float32), rh).reshape(ts * rh, 1))
    slope_col_c = jnp.asarray(np.tile(slopes_c.reshape(groups, 1, rh), (1, ts, 1)).reshape(groups, ts * rh, 1))
    pt_flat = page_table.reshape(-1)
    cache_dk = cache_diff_k.reshape(n_pool, page, 2 * dh)
    cache_dv = cache_diff_v.reshape(n_pool, page, dv)
    cache_mk = cache_moba_k.reshape(n_pool, page * groups, d_c)
    cache_mv = cache_moba_v.reshape(n_pool, page * groups, d_c)
    cache_mla_t = jnp.swapaxes(cache_mla, 1, 2)

    n_c = b + db
    c_all = jnp.pad(jnp.concatenate([c_prompt, c_sample], axis=0), ((0, -n_c % 16), (0, 0)))
    mod = _ada(c_all, w_ada, b_ada).reshape(w_ada.shape[0], c_all.shape[0], 6, d_model)
    mods = [(mod[l, :b], mod[l, b:n_c]) for l in range(mod.shape[0])]

    x = jnp.concatenate([x_prompt.reshape(mp, d_model), x_sample.reshape(ms, d_model)], axis=0)
    x = x.reshape(m // ts, ts, d_model)
    pp = functools.partial(_postpre, dims=dims)

    def ffn_mats(h, l):
        a = _matmul(h.reshape(m, d_model), w_ff_up, BF16, bm=1024, bn=512, epilogue="relu2", layer=l)
        return _matmul(a, w_ff_down, F32, bm=1024, bn=1024, bk=2048, layer=l).reshape(x.shape)

    (h,) = pp(x, None, None, None, mods[0], g_pre_mix[0], gate_idx=2, pre_idx=0)
    z0 = _matmul(h.reshape(m, d_model), w0, F32)
    q = _matmul(z0, w_q, F32, a_cols=(c_qa, q_rank), gain=g_q_a)
    qscale_a = (nope + rope) ** -0.5 * LOG2E
    qscale_d = dh ** -0.5 * LOG2E
    q_lat, q_pe = _qhead(q, wuk_t, cosf, sins, heads=heads_a, qscale=qscale_a)
    rows_f, rows_bf = _kvrows(z0, c_kv // (kv_rank + LANES), g_kv_a, cosf, sins, rank=kv_rank)

    ol_p = _mla_prompt(q_lat, q_pe, rows_bf, b=b, t=t, heads=heads_a, rank=kv_rank)
    od_p = _diff_prompt(jnp.asarray(slopes_b), z0, lam_in, g_diff_sub, b=b, t=t, heads=heads_b,
                        k_blk=c_dk // LANES, v_blk=c_dv // LANES, dv=dv, qscale=qscale_d)
    r_ab = ts * heads_a
    ol_s, od_s = _ab_sample(
        pt_flat,
        q_lat[mp:].reshape(db, r_ab, kv_rank), q_pe[mp:].reshape(db, r_ab, LANES),
        z0[mp:, :c_qa].reshape(db, r_ab, 2 * dh),
        _pad_new(rows_bf[mp:], db, ts), _pad_new(z0[mp:, c_dk:c_dv], db, ts), _pad_new(z0[mp:, c_dv:], db, ts),
        slope_col_b, t_col, lam_in, g_diff_sub, cache_mla_t, cache_dk, cache_dv,
        n_pages=n_pages, rank=kv_rank, rope=rope, qscale_d=qscale_d)
    o_lat = jnp.concatenate([ol_p, ol_s.reshape(ms, heads_a * kv_rank)], axis=0)
    o_d = jnp.concatenate([od_p, od_s.reshape(ms, heads_b * dv)], axis=0)
    o_cat = jnp.concatenate([_headmm(o_lat, wuv_t, BF16), o_d], axis=1)
    mix = _matmul(o_cat, w_out_ab, F32, bm=1024, bn=512)
    x, h = pp(x, mix.reshape(x.shape), mods[0], g_post_mix[0], mods[0], g_pre_ffn[0], gate_idx=2, pre_idx=3)
    x, h = pp(x, ffn_mats(h, 0), mods[0], g_post_ffn[0], mods[1], g_pre_mix[1], gate_idx=5, pre_idx=0)

    zc = _matmul(h.reshape(m, d_model), w_in_c, F32)
    nq_c = heads_c * d_c
    qscale_c = d_c ** -0.5 * LOG2E
    oc_p = _moba_prompt(jnp.asarray(slopes_c), zc, b=b, t=t, heads=heads_c, groups=groups, d=d_c,
                        k_blk=heads_c, v_blk=heads_c + groups, qscale=qscale_c)
    q_s = zc[mp:, :nq_c].reshape(db, ts, groups, rh, d_c).transpose(0, 2, 1, 3, 4).reshape(db, groups, ts * rh, d_c)
    oc_s = _moba_sample(pt_flat, q_s, _pad_new(zc[mp:, nq_c:nq_c + groups * d_c], db, ts),
                        _pad_new(zc[mp:, nq_c + groups * d_c:], db, ts), slope_col_c, t_col_c,
                        cache_mk, cache_mv, n_pages=n_pages, groups=groups, d=d_c, qscale=qscale_c)
    oc_s = oc_s.reshape(db, groups, ts, rh, d_c).transpose(0, 2, 1, 3, 4).reshape(ms, nq_c)
    mix = _matmul(jnp.concatenate([oc_p, oc_s], axis=0), w_out_c, F32, bm=1024, bn=512)
    x, h = pp(x, mix.reshape(x.shape), mods[1], g_post_mix[1], mods[1], g_pre_ffn[1], gate_idx=2, pre_idx=3)
    y = ffn_mats(h, 1)
    (y_p,) = pp(x, y, mods[1], g_post_ffn[1], None, None, gate_idx=5, pre_idx=0, part="prompt")
    (y_s,) = pp(x, y, mods[1], g_post_ffn[1], None, None, gate_idx=5, pre_idx=0, part="sample")

    kd = groups * d_c
    return (y_p.reshape(b, t, d_model), y_s.reshape(db, ts, d_model),
            rows_f[:mp, :row_w].reshape(b, t, row_w),
            z0[:mp, c_dk:c_dv].reshape(b, t, 1, 2 * dh), z0[:mp, c_dv:].reshape(b, t, 1, dv),
            zc[:mp, nq_c:nq_c + kd].reshape(b, t, groups, d_c), zc[:mp, nq_c + kd:].reshape(b, t, groups, d_c),
            rows_f[mp:, :row_w].reshape(db, ts, row_w),
            z0[mp:, c_dk:c_dv].reshape(db, ts, 1, 2 * dh), z0[mp:, c_dv:].reshape(db, ts, 1, dv),
            zc[mp:, nq_c:nq_c + kd].reshape(db, ts, groups, d_c), zc[mp:, nq_c + kd:].reshape(db, ts, groups, d_c))
```
